```python
import math
import jax
import jax.numpy as jnp
from jax import lax
import numpy as np

D_MODEL = 2048
BATCH = 4
SEQ = 4096
DEPTH = 1
DEC_BATCH = 128
DEC_SEQ = 4
PAST_LEN = 16384
PAGE_SIZE = 128

H_A = 8
DH_A = 128
BLK = 64
N_SEL = 16
WINDOW = 512
H_B = 8
D_NOPE = 128
D_ROPE = 64
D_V = 128
Q_LORA = 512
KV_LORA = 512
MLA_ROW = KV_LORA + D_ROPE
ROPE_THETA = 10000.0
N_BUCKETS = 32
T5_MAX_DIST = 128
D_FF = -(-8 * D_MODEL // (3 * 256)) * 256
MIX_A = H_A * DH_A
MIX_B = H_B * D_V
IN_SIZES = (H_A * DH_A, 4 * DH_A, 2 * DH_A, 3 * H_A, Q_LORA, KV_LORA, D_ROPE)
IN_COLS = sum(IN_SIZES)
SCALE_A = DH_A ** -0.5
SCALE_B = (D_NOPE + D_ROPE) ** -0.5
Q_BLOCK = 128
EPS = 1e-6
NEG_INF = -1e30
FORCED_SCORE = 1e4

kernel_name = 'hybrid_nsa_mla_adaln_decode_step'


def rmsnorm(x, g):
    xf = x.astype(jnp.float32)
    y = xf * lax.rsqrt(jnp.mean(xf * xf, axis=-1, keepdims=True) + EPS)
    return (y * g.astype(jnp.float32)).astype(x.dtype)


def rope(x, pos):
    half = x.shape[-1] // 2
    inv = ROPE_THETA ** (-jnp.arange(half, dtype=jnp.float32) / half)
    ang = pos.astype(jnp.float32)[:, None] * inv[None, :]
    shape = (pos.shape[0],) + (1,) * (x.ndim - 3) + (half,)
    cos = jnp.cos(ang).reshape(shape).astype(x.dtype)
    sin = jnp.sin(ang).reshape(shape).astype(x.dtype)
    x1, x2 = x[..., :half], x[..., half:]
    return jnp.concatenate([x1 * cos - x2 * sin, x1 * sin + x2 * cos], axis=-1)


def t5_bias(rel, table):
    n = jnp.maximum(rel, 0)
    exact = N_BUCKETS // 2
    nf = jnp.maximum(n, 1).astype(jnp.float32)
    far = exact + (jnp.log(nf / exact) / math.log(T5_MAX_DIST / exact) * (N_BUCKETS - exact)).astype(jnp.int32)
    bucket = jnp.where(n < exact, n, jnp.minimum(far, N_BUCKETS - 1))
    return jnp.moveaxis(table[bucket].astype(jnp.float32), -1, 0)


def softmax_masked(logits, mask):
    logits = jnp.where(mask, logits, NEG_INF)
    m = jnp.max(logits, axis=-1, keepdims=True)
    p = jnp.exp(logits - m) * mask
    return p / jnp.maximum(jnp.sum(p, axis=-1, keepdims=True), 1e-30)


def attend_seq(q_nsa, gates, nsa_kv, win_kv, q_lat, q_rope, mla_kv, rel_table, comp, q_pos0, win_pos0):
    w_ck1, w_ck2, w_cv1, w_cv2 = comp
    T = q_nsa.shape[0]
    L = nsa_kv.shape[0]
    dt = q_nsa.dtype
    nb_c = L // BLK
    ck = jax.nn.gelu(nsa_kv[: nb_c * BLK, 0].reshape(nb_c, BLK * DH_A) @ w_ck1) @ w_ck2
    cv = jax.nn.gelu(nsa_kv[: nb_c * BLK, 1].reshape(nb_c, BLK * DH_A) @ w_cv1) @ w_cv2
    blk_end = jnp.arange(nb_c, dtype=jnp.int32) * BLK + (BLK - 1)
    nb_s = -(-L // BLK)
    k_sel = min(N_SEL, nb_s)
    k_slc, v_slc = nsa_kv[:, 2], nsa_kv[:, 3]
    blk_ids = jnp.arange(nb_s, dtype=jnp.int32)
    out_a, out_b = [], []
    for s in range(0, T, Q_BLOCK):
        e = min(s + Q_BLOCK, T)
        qpos = q_pos0 + jnp.arange(s, e, dtype=jnp.int32)
        q = q_nsa[s:e]
        g = gates[s:e]
        lg = jnp.einsum('thd,bd->htb', q, ck).astype(jnp.float32) * SCALE_A + t5_bias(qpos[:, None] - blk_end[None, :], rel_table)
        p_cmp = softmax_masked(lg, (blk_end[None, :] <= qpos[:, None])[None])
        o_cmp = jnp.einsum('htb,bd->thd', p_cmp.astype(dt), cv)
        imp = jnp.pad(jnp.sum(p_cmp, axis=0), ((0, 0), (0, nb_s - nb_c)))
        cur = qpos // BLK
        forced = (blk_ids[None, :] == 0) | (blk_ids[None, :] == cur[:, None]) | (blk_ids[None, :] == cur[:, None] - 1)
        score = jnp.where(forced, FORCED_SCORE, jnp.where(blk_ids[None, :] <= cur[:, None], imp, -FORCED_SCORE))
        _, sel = lax.top_k(score, k_sel)
        tok = (sel[:, :, None] * BLK + jnp.arange(BLK, dtype=jnp.int32)).reshape(e - s, k_sel * BLK)
        tok_c = jnp.minimum(tok, L - 1)
        ks = k_slc[tok_c]
        vs = v_slc[tok_c]
        lg = jnp.einsum('thd,tkd->htk', q, ks).astype(jnp.float32) * SCALE_A + t5_bias(qpos[:, None] - tok, rel_table)
        p = softmax_masked(lg, (tok <= qpos[:, None])[None])
        o_slc = jnp.einsum('htk,tkd->thd', p.astype(dt), vs)
        ws = max(0, q_pos0 + s - WINDOW - win_pos0)
        we = q_pos0 + e - win_pos0
        kpos = win_pos0 + jnp.arange(ws, we, dtype=jnp.int32)
        rel = qpos[:, None] - kpos[None, :]
        lg = jnp.einsum('thd,kd->htk', q, win_kv[ws:we, 0]).astype(jnp.float32) * SCALE_A + t5_bias(rel, rel_table)
        p = softmax_masked(lg, ((rel >= 0) & (rel <= WINDOW))[None])
        o_win = jnp.einsum('htk,kd->thd', p.astype(dt), win_kv[ws:we, 1])
        out_a.append(g[..., 0:1] * o_cmp + g[..., 1:2] * o_slc + g[..., 2:3] * o_win)
        k_end = q_pos0 + e
        lat = mla_kv[:k_end, :KV_LORA]
        kr = mla_kv[:k_end, KV_LORA:]
        lg = (jnp.einsum('thr,kr->htk', q_lat[s:e], lat) + jnp.einsum('thr,kr->htk', q_rope[s:e], kr)).astype(jnp.float32) * SCALE_B
        p = softmax_masked(lg, (jnp.arange(k_end, dtype=jnp.int32)[None, :] <= qpos[:, None])[None])
        out_b.append(jnp.einsum('htk,kr->thr', p.astype(dt), lat))
    return jnp.concatenate(out_a, axis=0), jnp.concatenate(out_b, axis=0)


def decoder_layer(x, c, pos0, attend, g_attn, g_ffn, w_ada, b_ada, w_in, g_qnorm, w_uq, g_kvnorm, w_uk, w_uv, w_out, w_gate, w_up, w_down):
    N, T, _ = x.shape
    pos = pos0 + jnp.arange(T, dtype=jnp.int32)
    mod = jax.nn.silu(c) @ w_ada + b_ada
    sh_a, sc_a, gt_a, sh_f, sc_f, gt_f = jnp.split(mod[:, None, :], 6, axis=-1)
    h = rmsnorm(x, g_attn) * (1 + sc_a) + sh_a
    z = h @ w_in
    splits = [int(v) for v in np.cumsum(IN_SIZES)[:-1]]
    zq, zkv, zwin, zg, zcq, zckv, zkr = jnp.split(z, splits, axis=-1)
    q_nsa = zq.reshape(N, T, H_A, DH_A)
    nsa_rows = zkv.reshape(N, T, 4, DH_A)
    win_rows = zwin.reshape(N, T, 2, DH_A)
    gates = jax.nn.sigmoid(zg.reshape(N, T, H_A, 3))
    mla_rows = jnp.concatenate([rmsnorm(zckv, g_kvnorm), rope(zkr, pos)], axis=-1)
    q = (rmsnorm(zcq, g_qnorm) @ w_uq).reshape(N, T, H_B, D_NOPE + D_ROPE)
    q_lat = jnp.einsum('nthd,rhd->nthr', q[..., :D_NOPE], w_uk)
    q_rope = rope(q[..., D_NOPE:], pos)
    o_nsa, o_lat = attend(q_nsa, gates, nsa_rows, win_rows, q_lat, q_rope, mla_rows)
    o_mla = jnp.einsum('nthr,rhd->nthd', o_lat, w_uv)
    mixed = jnp.concatenate([o_nsa.reshape(N, T, MIX_A), o_mla.reshape(N, T, MIX_B)], axis=-1) @ w_out
    x = x + gt_a * mixed
    h = rmsnorm(x, g_ffn) * (1 + sc_f) + sh_f
    x = x + gt_f * ((jax.nn.silu(h @ w_gate) * (h @ w_up)) @ w_down)
    return x, nsa_rows, mla_rows, win_rows


def setup_inputs(seed: int = 0) -> dict:
    key = jax.random.key(seed)
    ks = jax.random.split(key, 32)
    f32 = jnp.float32
    n_pages = PAST_LEN // PAGE_SIZE
    n_pool = (5 * DEC_BATCH * n_pages + 3) // 4
    wb = min(WINDOW, PAST_LEN)

    def nrm(i, shape, scale=1.0):
        return jax.random.normal(ks[i], shape, f32) * scale

    def gain(i, n):
        return 1.0 + nrm(i, (n,), 0.05)

    perm = jax.random.permutation(ks[0], n_pool)
    page_table = perm[: DEC_BATCH * n_pages].reshape(DEC_BATCH, n_pages).astype(jnp.int32)
    return {
        'x_prompt': nrm(1, (BATCH, SEQ, D_MODEL)),
        'x_sample': nrm(2, (DEC_BATCH, DEC_SEQ, D_MODEL)),
        'cache_nsa_kv': nrm(3, (n_pool, PAGE_SIZE, 4, DH_A)),
        'cache_mla': nrm(4, (n_pool, PAGE_SIZE, MLA_ROW)),
        'state_win_kv': nrm(5, (DEC_BATCH, wb, 2, DH_A)),
        'page_table': page_table,
        'c_prompt': nrm(6, (BATCH, D_MODEL)),
        'c_sample': nrm(7, (DEC_BATCH, D_MODEL)),
        'rel_table': nrm(8, (N_BUCKETS, H_A), 0.5),
        'g_attn': gain(9, D_MODEL),
        'g_ffn': gain(10, D_MODEL),
        'g_final': gain(11, D_MODEL),
        'w_ada': nrm(12, (D_MODEL, 6 * D_MODEL), 0.5 * D_MODEL ** -0.5),
        'b_ada': nrm(13, (6 * D_MODEL,), 0.02),
        'w_in': nrm(14, (D_MODEL, IN_COLS), D_MODEL ** -0.5),
        'g_qnorm': gain(15, Q_LORA),
        'w_uq': nrm(16, (Q_LORA, H_B * (D_NOPE + D_ROPE)), Q_LORA ** -0.5),
        'g_kvnorm': gain(17, KV_LORA),
        'w_uk': nrm(18, (KV_LORA, H_B, D_NOPE), KV_LORA ** -0.5),
        'w_uv': nrm(19, (KV_LORA, H_B, D_V), KV_LORA ** -0.5),
        'w_ck1': nrm(20, (BLK * DH_A, DH_A), (BLK * DH_A) ** -0.5),
        'w_ck2': nrm(21, (DH_A, DH_A), DH_A ** -0.5),
        'w_cv1': nrm(22, (BLK * DH_A, DH_A), (BLK * DH_A) ** -0.5),
        'w_cv2': nrm(23, (DH_A, DH_A), DH_A ** -0.5),
        'w_out': nrm(24, (MIX_A + MIX_B, D_MODEL), (MIX_A + MIX_B) ** -0.5),
        'w_gate': nrm(25, (D_MODEL, D_FF), D_MODEL ** -0.5),
        'w_up': nrm(26, (D_MODEL, D_FF), D_MODEL ** -0.5),
        'w_down': nrm(27, (D_FF, D_MODEL), D_FF ** -0.5),
    }


def reference(x_prompt, x_sample, cache_nsa_kv, cache_mla, state_win_kv, page_table, c_prompt, c_sample,
              rel_table, g_attn, g_ffn, g_final, w_ada, b_ada, w_in, g_qnorm, w_uq, g_kvnorm, w_uk, w_uv,
              w_ck1, w_ck2, w_cv1, w_cv2, w_out, w_gate, w_up, w_down):
    comp = (w_ck1, w_ck2, w_cv1, w_cv2)
    layer_w = (g_attn, g_ffn, w_ada, b_ada, w_in, g_qnorm, w_uq, g_kvnorm, w_uk, w_uv, w_out, w_gate, w_up, w_down)
    past = page_table.shape[1] * PAGE_SIZE
    wb = state_win_kv.shape[1]

    def attend_prompt(q_nsa, gates, nsa_rows, win_rows, q_lat, q_rope, mla_rows):
        def one(qn, gg, kv, wk, ql, qr, mk):
            return attend_seq(qn, gg, kv, wk, ql, qr, mk, rel_table, comp, 0, 0)
        return jax.vmap(one)(q_nsa, gates, nsa_rows, win_rows, q_lat, q_rope, mla_rows)

    def attend_sample(q_nsa, gates, nsa_rows, win_rows, q_lat, q_rope, mla_rows):
        def one(args):
            qn, gg, kv_new, win_new, ql, qr, mla_new, pages, win_buf = args
            kv = jnp.concatenate([cache_nsa_kv[pages].reshape(past, 4, DH_A), kv_new], axis=0)
            mk = jnp.concatenate([cache_mla[pages].reshape(past, MLA_ROW), mla_new], axis=0)
            wk = jnp.concatenate([win_buf, win_new], axis=0)
            return attend_seq(qn, gg, kv, wk, ql, qr, mk, rel_table, comp, past, past - wb)
        return lax.map(one, (q_nsa, gates, nsa_rows, win_rows, q_lat, q_rope, mla_rows, page_table, state_win_kv))

    h_p, h_s = x_prompt, x_sample
    for _ in range(DEPTH):
        h_p, nsa_p, mla_p, win_rows_p = decoder_layer(h_p, c_prompt, 0, attend_prompt, *layer_w)
        h_s, nsa_s, mla_s, win_rows_s = decoder_layer(h_s, c_sample, past, attend_sample, *layer_w)
    y_prompt = rmsnorm(h_p, g_final)
    y_sample = rmsnorm(h_s, g_final)
    win_p = win_rows_p[:, -min(WINDOW, win_rows_p.shape[1]):]
    win_all = jnp.concatenate([state_win_kv, win_rows_s], axis=1)
    win_s = win_all[:, -min(WINDOW, win_all.shape[1]):]
    return (y_prompt, y_sample, nsa_p, mla_p, win_p, nsa_s, mla_s, win_s)
```

```python
import functools
import math

import numpy as np
import jax
import jax.numpy as jnp
from jax import lax
from jax.experimental import pallas as pl
from jax.experimental.pallas import tpu as pltpu

F32 = jnp.float32
BF16 = jnp.bfloat16
I32 = jnp.int32

H_A = 8
DH_A = 128
BLK = 64
N_SEL = 16
WINDOW = 512
H_B = 8
D_NOPE = 128
D_ROPE = 64
D_V = 128
Q_LORA = 512
KV_LORA = 512
MLA_ROW = KV_LORA + D_ROPE
PAGE_SIZE = 128
ROPE_THETA = 10000.0
N_BUCKETS = 32
T5_MAX_DIST = 128
SCALE_A = DH_A ** -0.5
SCALE_B = (D_NOPE + D_ROPE) ** -0.5
EPS = 1e-6
NEG_INF = -1e30
FORCED_SCORE = 1e4

Q_TILE = 128
K_TILE = 512
LANES = 128
V7X_VMEM_LIMIT = 56 * 1024 * 1024
PAGE_PITCH = PAGE_SIZE + 8
MLA_CHUNK_PAGES = 16
MLA_RING = 3


def _t5_bucket_np(n):
    n = np.maximum(np.asarray(n, np.int64), 0)
    exact = N_BUCKETS // 2
    nf = np.maximum(n, 1).astype(np.float32)
    far = exact + (np.log(nf / np.float32(exact)) / np.float32(math.log(T5_MAX_DIST / exact))
                   * np.float32(N_BUCKETS - exact)).astype(np.int32)
    return np.where(n < exact, n, np.minimum(far, N_BUCKETS - 1)).astype(np.int32)


_BUCKET_START = [int(np.argmax(_t5_bucket_np(np.arange(4 * T5_MAX_DIST)) >= b)) for b in range(N_BUCKETS)]


def _dot(a, b):
    return jnp.dot(a, b, preferred_element_type=F32)


def _dot_nt(a, b):
    return lax.dot_general(a, b, (((1,), (1,)), ((), ())), preferred_element_type=F32)


def _rms(x, g):
    return x * lax.rsqrt(jnp.mean(x * x, axis=-1, keepdims=True) + EPS) * g


def _cparams(sem, vmem=V7X_VMEM_LIMIT):
    return pltpu.CompilerParams(dimension_semantics=sem, vmem_limit_bytes=vmem)


def _const_spec(shape):
    nd = len(shape)
    return pl.BlockSpec(shape, lambda *_: (0,) * nd, pipeline_mode=pl.Buffered(1))


def _mod_kernel(c_ref, w_ref, b_ref, o_ref):
    c = c_ref[...]
    s = (c * jax.nn.sigmoid(c)).astype(BF16)
    o_ref[...] = _dot(s, w_ref[...].astype(BF16)) + b_ref[...]


def _modulation(c_all, w_ada, b_ada):
    m, d = c_all.shape
    n = w_ada.shape[1]
    tn = 512
    return pl.pallas_call(
        _mod_kernel,
        grid=(n // tn,),
        in_specs=[pl.BlockSpec((m, d), lambda j: (0, 0)),
                  pl.BlockSpec((d, tn), lambda j: (0, j)),
                  pl.BlockSpec((1, tn), lambda j: (0, j))],
        out_specs=pl.BlockSpec((m, tn), lambda j: (0, j)),
        out_shape=jax.ShapeDtypeStruct((m, n), F32),
        compiler_params=_cparams(("arbitrary",)),
        name="adaln_mod",
    )(c_all, w_ada, b_ada.reshape(1, n))


C_QN = 0
C_KV = C_QN + H_A * DH_A
C_WIN = C_KV + 4 * DH_A
C_CQ = C_WIN + 2 * DH_A
C_CKV = C_CQ + Q_LORA
C_KR = C_CKV + KV_LORA
C_KRR = C_KR + LANES
C_G = C_KRR + LANES
C_END = C_G + LANES


def _inproj_kernel(x_ref, sc_ref, sh_ref, g_ref, w_ref, gq_ref, wuq_ref, gkv_ref, wkv_ref, cos_ref, sin_ref,
                   qn_ref, nsa_ref, win_ref, gates_ref, mla_ref, qnope_ref, qrope_ref, krp_ref, knope_ref, v_ref):
    x = x_ref[...]
    h = _rms(x, g_ref[...]) * (1.0 + sc_ref[...]) + sh_ref[...]
    z = _dot(h.astype(BF16), w_ref[...])
    qn_ref[...] = z[:, C_QN:C_KV].astype(BF16)
    nsa_ref[...] = z[:, C_KV:C_WIN]
    win_ref[...] = z[:, C_WIN:C_CQ]
    gates_ref[...] = jax.nn.sigmoid(z[:, C_G:C_END])
    cos = cos_ref[...]
    sin = sin_ref[...]
    kvn = _rms(z[:, C_CKV:C_KR], gkv_ref[...])
    krp = z[:, C_KR:C_KRR] * cos + z[:, C_KRR:C_G] * sin
    mla_ref[:, 0:KV_LORA] = kvn
    mla_ref[:, KV_LORA:MLA_ROW] = krp[:, 0:D_ROPE]
    krp_ref[...] = krp.astype(BF16)
    cqn = _rms(z[:, C_CQ:C_CKV], gq_ref[...])
    q = _dot(cqn.astype(BF16), wuq_ref[...])
    nq = H_B * D_NOPE
    qnope_ref[...] = q[:, 0:nq].astype(BF16)
    for hh in range(H_B):
        a = q[:, nq + hh * LANES: nq + (hh + 1) * LANES]
        b = q[:, 2 * nq + hh * LANES: 2 * nq + (hh + 1) * LANES]
        qrope_ref[:, hh * LANES:(hh + 1) * LANES] = (a * cos + b * sin).astype(BF16)
    kv = _dot(kvn.astype(BF16), wkv_ref[...])
    knope_ref[...] = kv[:, 0:nq].astype(BF16)
    v_ref[...] = kv[:, nq:2 * nq].astype(BF16)


def _inproj(x2d, sc, sh, g_attn, w_ext, g_q, wuq_ext, g_kv, wkv, cos_t, sin_t, rows_per_group):
    rows, d = x2d.shape
    tm = min(512, rows)
    nt = rows // tm
    per_row_mod = sc.shape[1] != 1
    if per_row_mod:
        mod_spec = pl.BlockSpec((None, tm, d), lambda i: (0, i, 0))
    else:
        tps = rows_per_group // tm
        mod_spec = pl.BlockSpec((None, 1, d), lambda i: (i // tps, 0, 0))
    ptiles = cos_t.shape[0] // tm
    pos_spec = pl.BlockSpec((tm, LANES), lambda i: (i % ptiles, 0))
    row = lambda n: pl.BlockSpec((tm, n), lambda i: (i, 0))
    nq = H_B * D_NOPE
    outs = [(H_A * DH_A, BF16), (4 * DH_A, F32), (2 * DH_A, F32), (LANES, F32), (MLA_ROW, F32),
            (nq, BF16), (H_B * LANES, BF16), (LANES, BF16), (nq, BF16), (nq, BF16)]
    return pl.pallas_call(
        _inproj_kernel,
        grid=(nt,),
        in_specs=[row(d), mod_spec, mod_spec, _const_spec((1, d)), _const_spec(w_ext.shape),
                  _const_spec((1, Q_LORA)), _const_spec(wuq_ext.shape), _const_spec((1, KV_LORA)),
                  _const_spec(wkv.shape), pos_spec, pos_spec],
        out_specs=[row(n) for n, _ in outs],
        out_shape=[jax.ShapeDtypeStruct((rows, n), dt) for n, dt in outs],
        compiler_params=_cparams(("arbitrary",)),
        name="inproj",
    )(x2d, sc, sh, g_attn.reshape(1, d), w_ext, g_q.reshape(1, Q_LORA), wuq_ext, g_kv.reshape(1, KV_LORA),
      wkv, cos_t, sin_t)


def _compress_kernel(kb_ref, vb_ref, w1k_ref, w2k_ref, w1v_ref, w2v_ref, ck_ref, cv_ref):
    hk = jax.nn.gelu(_dot(kb_ref[...].astype(BF16), w1k_ref[...]))
    ck_ref[...] = _dot(hk.astype(BF16), w2k_ref[...])
    hv = jax.nn.gelu(_dot(vb_ref[...].astype(BF16), w1v_ref[...]))
    cv_ref[...] = _dot(hv.astype(BF16), w2v_ref[...])


def _compress_prompt(kb, vb, w1k, w2k, w1v, w2v):
    nblk = kb.shape[0]
    full = lambda a: pl.BlockSpec(a.shape, lambda i: (0,) * a.ndim)
    args = (kb, vb, w1k, w2k, w1v, w2v)
    return pl.pallas_call(
        _compress_kernel,
        grid=(1,),
        in_specs=[full(a) for a in args],
        out_specs=[pl.BlockSpec((nblk, DH_A), lambda i: (0, 0))] * 2,
        out_shape=[jax.ShapeDtypeStruct((nblk, DH_A), F32)] * 2,
        compiler_params=_cparams(("arbitrary",)),
        name="compress_prompt",
    )(*args)


def _flash_update(carry, s, v_t):
    m, l, acc = carry
    m_new = jnp.maximum(m, jnp.max(s, axis=0, keepdims=True))
    alpha = jnp.exp(m - m_new)
    p = jnp.exp(s - m_new)
    l = alpha * l + jnp.sum(p, axis=0, keepdims=True)
    acc = alpha * acc + _dot(v_t, p.astype(BF16))
    return m_new, l, acc


def _flash_init(d):
    return (jnp.full((1, Q_TILE), NEG_INF, F32), jnp.zeros((1, Q_TILE), F32), jnp.zeros((d, Q_TILE), F32))


def _nsa_prompt_kernel(q_ref, gt_ref, ck_ref, cvt_ref, bct_ref, band_ref, c31_ref,
                       ks_ref, vst_ref, kw_ref, vwt_ref, o_ref,
                       acc_ref, score_ref, self_ref, selr_ref, madd_ref, nearm_ref, *, nb, k_sel):
    qb = pl.program_id(1)
    qs = qb * Q_TILE
    t_iota = lax.broadcasted_iota(I32, (1, Q_TILE), 1)
    tpos = qs + t_iota
    b_iota = lax.broadcasted_iota(I32, (nb, 1), 0)
    blk_end = b_iota * BLK + (BLK - 1)
    valid_c = blk_end <= tpos
    valid_cf = valid_c.astype(F32)
    ck = ck_ref[...]
    cvt = cvt_ref[...]

    imp = jnp.zeros((nb, Q_TILE), F32)
    for h in range(H_A):
        qh = q_ref[:, h * DH_A:(h + 1) * DH_A]
        lg = _dot_nt(ck, qh) * SCALE_A + bct_ref[h]
        lg = jnp.where(valid_c, lg, NEG_INF)
        m = jnp.max(lg, axis=0, keepdims=True)
        p = jnp.exp(lg - m) * valid_cf
        p = p / jnp.maximum(jnp.sum(p, axis=0, keepdims=True), 1e-30)
        imp = imp + p
        acc_ref[h] = gt_ref[3 * h:3 * h + 1, :] * _dot(cvt, p.astype(BF16))

    cur = tpos // BLK
    forced = (b_iota == 0) | (b_iota == cur) | (b_iota == cur - 1)
    score = jnp.where(forced, FORCED_SCORE, jnp.where(b_iota <= cur, imp, -FORCED_SCORE))
    score_ref[...] = score

    def rank_body(bp, rank):
        row = score_ref[pl.ds(bp, 1), :]
        tie = jnp.where(bp < b_iota, 1, 0)
        return rank + jnp.where(row > score, 1, jnp.where(row == score, tie, 0))

    rank = lax.fori_loop(0, nb, rank_body, jnp.zeros((nb, Q_TILE), I32))
    sel = rank < k_sel
    selr_ref[...] = sel.astype(F32)
    self_ref[...] = jnp.where(sel & (b_iota < 2 * qb - 2), 0.0, NEG_INF)
    nfar = (jnp.maximum(qb - 1, 0) + 3) // 4
    blocks_per_tile = K_TILE // BLK

    def mask_body(j, c):
        for i in range(blocks_per_tile):
            row = self_ref[pl.ds(j * blocks_per_tile + i, 1), :]
            madd_ref[pl.ds(pl.multiple_of(j * K_TILE + i * BLK, BLK), BLK), :] = jnp.broadcast_to(row, (BLK, Q_TILE))
        return c

    lax.fori_loop(0, nfar, mask_body, 0)

    kp_iota = lax.broadcasted_iota(I32, (2 * Q_TILE, 1), 0)
    causal = (kp_iota - Q_TILE) <= t_iota
    in_seq = (kp_iota + qs) >= Q_TILE
    for i in range(4):
        bi = 2 * qb - 2 + i
        row = selr_ref[pl.ds(jnp.maximum(bi, 0), 1), :]
        ok = (row > 0.5) & causal[i * BLK:(i + 1) * BLK] & (bi >= 0)
        nearm_ref[i * BLK:(i + 1) * BLK, :] = jnp.where(ok, 0.0, NEG_INF)
    win_near = jnp.where(causal & in_seq, 0.0, NEG_INF)
    nwf = WINDOW - Q_TILE
    wf_iota = lax.broadcasted_iota(I32, (nwf, 1), 0)
    win_far = jnp.where((wf_iota >= t_iota) & ((wf_iota + qs) >= WINDOW), 0.0, NEG_INF)

    near0 = pl.multiple_of(qs, Q_TILE)
    for h in range(H_A):
        qh = q_ref[:, h * DH_A:(h + 1) * DH_A]
        c31 = c31_ref[h]
        band = band_ref[h]

        def far_body(j, carry):
            r0 = pl.multiple_of(Q_TILE + j * K_TILE, Q_TILE)
            s = _dot_nt(ks_ref[pl.ds(r0, K_TILE), :], qh) * SCALE_A + c31
            s = s + madd_ref[pl.ds(pl.multiple_of(j * K_TILE, K_TILE), K_TILE), :]
            return _flash_update(carry, s, vst_ref[:, pl.ds(r0, K_TILE)])

        carry = lax.fori_loop(0, nfar, far_body, _flash_init(DH_A))
        s = _dot_nt(ks_ref[pl.ds(near0, 2 * Q_TILE), :], qh) * SCALE_A + band + nearm_ref[...]
        m, l, acc = _flash_update(carry, s, vst_ref[:, pl.ds(near0, 2 * Q_TILE)])
        acc_ref[h] += gt_ref[3 * h + 1:3 * h + 2, :] * (acc / jnp.maximum(l, 1e-30))

        s = _dot_nt(kw_ref[pl.ds(near0, nwf), :], qh) * SCALE_A + c31 + win_far
        carry = _flash_update(_flash_init(DH_A), s, vwt_ref[:, pl.ds(near0, nwf)])
        wn0 = pl.multiple_of(qs + nwf, Q_TILE)
        s = _dot_nt(kw_ref[pl.ds(wn0, 2 * Q_TILE), :], qh) * SCALE_A + band + win_near
        m, l, acc = _flash_update(carry, s, vwt_ref[:, pl.ds(wn0, 2 * Q_TILE)])
        tot = acc_ref[h] + gt_ref[3 * h + 2:3 * h + 3, :] * (acc / jnp.maximum(l, 1e-30))
        o_ref[:, h * DH_A:(h + 1) * DH_A] = tot.T.astype(BF16)


def _nsa_prompt(qn, gates_t, ck, cvt, bias_ct, band_t, c31, ks_p, vst_p, kw_p, vwt_p, n_seq, seq):
    nb = seq // BLK
    nq = seq // Q_TILE
    k_sel = min(N_SEL, nb)
    per_seq = lambda a: pl.BlockSpec((None,) + a.shape[1:], lambda n, i: (n,) + (0,) * (a.ndim - 1),
                                     pipeline_mode=pl.Buffered(1))
    kern = functools.partial(_nsa_prompt_kernel, nb=nb, k_sel=k_sel)
    return pl.pallas_call(
        kern,
        grid=(n_seq, nq),
        in_specs=[pl.BlockSpec((Q_TILE, H_A * DH_A), lambda n, i: (n * nq + i, 0)),
                  pl.BlockSpec((None, 3 * H_A, Q_TILE), lambda n, i: (n, 0, i)),
                  per_seq(ck), per_seq(cvt),
                  pl.BlockSpec((H_A, nb, Q_TILE), lambda n, i: (0, 0, i)),
                  _const_spec(band_t.shape), _const_spec(c31.shape),
                  per_seq(ks_p), per_seq(vst_p), per_seq(kw_p), per_seq(vwt_p)],
        out_specs=pl.BlockSpec((Q_TILE, H_A * DH_A), lambda n, i: (n * nq + i, 0)),
        out_shape=jax.ShapeDtypeStruct((n_seq * seq, H_A * DH_A), BF16),
        scratch_shapes=[pltpu.VMEM((H_A, DH_A, Q_TILE), F32), pltpu.VMEM((nb, Q_TILE), F32),
                        pltpu.VMEM((nb, Q_TILE), F32), pltpu.VMEM((nb, Q_TILE), F32),
                        pltpu.VMEM((seq, Q_TILE), F32), pltpu.VMEM((2 * Q_TILE, Q_TILE), F32)],
        compiler_params=_cparams(("arbitrary", "arbitrary")),
        name="nsa_prompt",
    )(qn, gates_t, ck, cvt, bias_ct, band_t, c31, ks_p, vst_p, kw_p, vwt_p)


def _mla_prompt_kernel(qn_ref, qr_ref, kn_ref, kr_ref, vt_ref, o_ref, *, tk):
    qb = pl.program_id(1)
    qs = qb * Q_TILE
    nfull = qs // tk
    t_iota = lax.broadcasted_iota(I32, (1, Q_TILE), 1)
    k_iota = lax.broadcasted_iota(I32, (tk, 1), 0)
    diag0 = pl.multiple_of(nfull * tk, tk)
    diag_mask = jnp.where((k_iota + diag0) <= (t_iota + qs), 0.0, NEG_INF)
    for h in range(H_B):
        qnh = qn_ref[:, h * D_NOPE:(h + 1) * D_NOPE]
        qrh = qr_ref[:, h * LANES:(h + 1) * LANES]

        def logits(r0):
            s = _dot_nt(kn_ref[pl.ds(r0, tk), h * D_NOPE:(h + 1) * D_NOPE], qnh)
            s = s + _dot_nt(kr_ref[pl.ds(r0, tk), :], qrh)
            return s * SCALE_B

        def body(j, carry):
            r0 = pl.multiple_of(j * tk, tk)
            return _flash_update(carry, logits(r0), vt_ref[h * D_V:(h + 1) * D_V, pl.ds(r0, tk)])

        carry = lax.fori_loop(0, nfull, body, _flash_init(D_V))
        m, l, acc = _flash_update(carry, logits(diag0) + diag_mask, vt_ref[h * D_V:(h + 1) * D_V, pl.ds(diag0, tk)])
        o_ref[:, h * D_V:(h + 1) * D_V] = (acc / jnp.maximum(l, 1e-30)).T.astype(BF16)


def _mla_prompt(qnope, qrope, knope, krp, vt, n_seq, seq):
    nq = seq // Q_TILE
    tk = min(K_TILE, seq)
    per_seq = lambda a: pl.BlockSpec((None,) + a.shape[1:], lambda n, i: (n,) + (0,) * (a.ndim - 1),
                                     pipeline_mode=pl.Buffered(1))
    qspec = lambda w: pl.BlockSpec((Q_TILE, w), lambda n, i: (n * nq + i, 0))
    return pl.pallas_call(
        functools.partial(_mla_prompt_kernel, tk=tk),
        grid=(n_seq, nq),
        in_specs=[qspec(H_B * D_NOPE), qspec(H_B * LANES), per_seq(knope), per_seq(krp), per_seq(vt)],
        out_specs=qspec(H_B * D_V),
        out_shape=jax.ShapeDtypeStruct((n_seq * seq, H_B * D_V), BF16),
        compiler_params=_cparams(("arbitrary", "arbitrary")),
        name="mla_prompt",
    )(qnope, qrope, knope, krp, vt)


def _outproj_kernel(oa_ref, ob_ref, w_ref, x_ref, gt_ref, sc_ref, sh_ref, g_ref, x1_ref, h_ref):
    half = oa_ref.shape[1]
    mixed = _dot(oa_ref[...], w_ref[0:half, :]) + _dot(ob_ref[...], w_ref[half:2 * half, :])
    x1 = x_ref[...] + gt_ref[...] * mixed
    x1_ref[...] = x1
    h_ref[...] = (_rms(x1, g_ref[...]) * (1.0 + sc_ref[...]) + sh_ref[...]).astype(BF16)


def _mod_spec(arr, tm, rows_per_group):
    d = arr.shape[-1]
    if arr.shape[1] != 1:
        return pl.BlockSpec((None, tm, d), lambda i, *_: (0, i, 0))
    tps = rows_per_group // tm
    return pl.BlockSpec((None, 1, d), lambda i, *_: (i // tps, 0, 0))


def _outproj(oa, ob, w_out, x2d, gt, sc, sh, g_ffn, rows_per_group):
    rows, d = x2d.shape
    tm = min(512, rows)
    row = lambda n: pl.BlockSpec((tm, n), lambda i: (i, 0))
    ms = _mod_spec(gt, tm, rows_per_group)
    return pl.pallas_call(
        _outproj_kernel,
        grid=(rows // tm,),
        in_specs=[row(oa.shape[1]), row(ob.shape[1]), _const_spec(w_out.shape), row(d), ms, ms, ms,
                  _const_spec((1, d))],
        out_specs=[row(d), row(d)],
        out_shape=[jax.ShapeDtypeStruct((rows, d), F32), jax.ShapeDtypeStruct((rows, d), BF16)],
        compiler_params=_cparams(("arbitrary",)),
        name="outproj",
    )(oa, ob, w_out, x2d, gt, sc, sh, g_ffn.reshape(1, d))


def _ffn_kernel(h_ref, wg_ref, wu_ref, wd_ref, x1_ref, gt_ref, g_ref, y_ref, acc_ref):
    j = pl.program_id(1)
    h = h_ref[...]
    a = _dot(h, wg_ref[...])
    u = _dot(h, wu_ref[...])
    part = _dot(((a * jax.nn.sigmoid(a)) * u).astype(BF16), wd_ref[...])

    @pl.when(j == 0)
    def _():
        acc_ref[...] = part

    @pl.when(j > 0)
    def _():
        acc_ref[...] += part

    @pl.when(j == pl.num_programs(1) - 1)
    def _():
        x2 = x1_ref[...] + gt_ref[...] * acc_ref[...]
        y_ref[...] = _rms(x2, g_ref[...])


def _ffn(h2, wg, wu, wd, x1, gt, g_final, rows_per_group):
    rows, d = x1.shape
    dff = wg.shape[1]
    tm = min(512, rows)
    tf = 512
    row = lambda n: pl.BlockSpec((tm, n), lambda i, j: (i, 0))
    return pl.pallas_call(
        _ffn_kernel,
        grid=(rows // tm, dff // tf),
        in_specs=[row(d), pl.BlockSpec((d, tf), lambda i, j: (0, j)), pl.BlockSpec((d, tf), lambda i, j: (0, j)),
                  pl.BlockSpec((tf, d), lambda i, j: (j, 0)), row(d), _mod_spec(gt, tm, rows_per_group),
                  pl.BlockSpec((1, d), lambda i, j: (0, 0))],
        out_specs=row(d),
        out_shape=jax.ShapeDtypeStruct((rows, d), F32),
        scratch_shapes=[pltpu.VMEM((tm, d), F32)],
        compiler_params=_cparams(("arbitrary", "arbitrary")),
        name="ffn",
    )(h2, wg, wu, wd, x1, gt, g_final.reshape(1, d))


def _headproj_kernel(x_ref, w_ref, o_ref):
    o_ref[...] = _dot(x_ref[...].astype(BF16), w_ref[...])


def _headproj(x, w):
    rows = x.shape[0]
    nh, k, n = w.shape
    return pl.pallas_call(
        _headproj_kernel,
        grid=(nh,),
        in_specs=[pl.BlockSpec((rows, k), lambda h: (0, h)), pl.BlockSpec((None, k, n), lambda h: (h, 0, 0))],
        out_specs=pl.BlockSpec((rows, n), lambda h: (0, h)),
        out_shape=jax.ShapeDtypeStruct((rows, nh * n), F32),
        compiler_params=_cparams(("arbitrary",)),
        name="headproj",
    )(x, w)


def _t5_bias_rows(n, tbl_ref):
    bias = jnp.broadcast_to(tbl_ref[:, N_BUCKETS - 1:N_BUCKETS], (tbl_ref.shape[0], n.shape[1]))
    for b in range(N_BUCKETS - 2, -1, -1):
        bias = jnp.where(n < _BUCKET_START[b + 1], tbl_ref[:, b:b + 1], bias)
    return bias


def _page_copies(cache_ref, pt_ref, stage_ref, sem_ref, seq, slot, npages):
    def one(p):
        page = pt_ref[seq, p]
        r0 = pl.multiple_of(p * PAGE_PITCH, 8)
        return [pltpu.make_async_copy(cache_ref.at[page, :, pl.ds(c * DH_A, DH_A)],
                                      stage_ref.at[slot, c, pl.ds(r0, PAGE_SIZE), :], sem_ref.at[slot])
                for c in range(2)]
    return one


def _cmp_sample_kernel(pt_ref, cache_ref, q_ref, w1k_ref, w2k_ref, w1v_ref, w2v_ref, bias_ref, bid_ref,
                       ocmp_ref, sel_ref, stage_ref, sem_ref, *, npages, k_sel, nbs):
    b = pl.program_id(0)
    nseq = pl.num_programs(0)
    slot = b % 2

    def start_all(seq, sl):
        one = _page_copies(cache_ref, pt_ref, stage_ref, sem_ref, seq, sl, npages)

        def body(p, c):
            for cp in one(p):
                cp.start()
            return c
        lax.fori_loop(0, npages, body, 0)

    @pl.when(b == 0)
    def _():
        start_all(0, 0)

    @pl.when(b + 1 < nseq)
    def _():
        start_all(b + 1, 1 - slot)

    one = _page_copies(cache_ref, pt_ref, stage_ref, sem_ref, b, slot, npages)

    def wait_body(p, c):
        for cp in one(p):
            cp.wait()
        return c
    lax.fori_loop(0, npages, wait_body, 0)

    nrow = 2 * npages

    def comp_body(j, carry):
        pk, pv = carry
        parts = []
        for c in range(2):
            ev = stage_ref[slot, c, pl.ds(j, npages, stride=PAGE_PITCH), :]
            od = stage_ref[slot, c, pl.ds(j + BLK, npages, stride=PAGE_PITCH), :]
            parts.append(jnp.concatenate([ev, od], axis=0).astype(BF16))
        return pk + _dot(parts[0], w1k_ref[j]), pv + _dot(parts[1], w1v_ref[j])

    zero = jnp.zeros((nrow, DH_A), F32)
    pk, pv = lax.fori_loop(0, BLK, comp_body, (zero, zero))
    ck = _dot(jax.nn.gelu(pk).astype(BF16), w2k_ref[...]).astype(BF16)
    cv = _dot(jax.nn.gelu(pv).astype(BF16), w2v_ref[...]).astype(BF16)

    q = q_ref[...]
    lg = _dot_nt(q, ck) * SCALE_A + bias_ref[...]
    m = jnp.max(lg, axis=-1, keepdims=True)
    p = jnp.exp(lg - m)
    p = p / jnp.maximum(jnp.sum(p, axis=-1, keepdims=True), 1e-30)
    ocmp_ref[...] = _dot(p.astype(BF16), cv)

    nt = q.shape[0] // H_A
    bid = bid_ref[...]
    width = bid.shape[1]
    imp = jnp.concatenate([jnp.sum(p[t * H_A:(t + 1) * H_A], axis=0, keepdims=True) for t in range(nt)], axis=0)
    imp = jnp.concatenate([imp, jnp.zeros((nt, width - nrow), F32)], axis=1)
    cur = float(nbs - 1)
    forced = (bid == 0.0) | (bid == cur) | (bid == cur - 1.0)
    score = jnp.where(forced, FORCED_SCORE, imp)
    score = jnp.where(bid < 0.0, NEG_INF, score)
    col = lax.broadcasted_iota(I32, (nt, LANES), 1)
    sel = jnp.zeros((nt, LANES), F32)
    for r in range(k_sel):
        mx = jnp.max(score, axis=-1, keepdims=True)
        idx = jnp.min(jnp.where(score == mx, bid, 1e9), axis=-1, keepdims=True)
        sel = jnp.where(col == r, idx, sel)
        score = jnp.where(bid == idx, NEG_INF, score)
    sel_ref[...] = sel.astype(I32)


def _cmp_sample(page_table, cache3, q32, w1k, w2k, w1v, w2v, bias_c, bid, nbs):
    nseq, npages = page_table.shape
    nrow = 2 * npages
    rows = q32.shape[1]
    nt = rows // H_A
    k_sel = min(N_SEL, nbs)
    kern = functools.partial(_cmp_sample_kernel, npages=npages, k_sel=k_sel, nbs=nbs)
    cst = lambda a: pl.BlockSpec(a.shape, lambda b, pt: (0,) * a.ndim, pipeline_mode=pl.Buffered(1))
    grid_spec = pltpu.PrefetchScalarGridSpec(
        num_scalar_prefetch=1,
        grid=(nseq,),
        in_specs=[pl.BlockSpec(memory_space=pl.ANY),
                  pl.BlockSpec((None, rows, DH_A), lambda b, pt: (b, 0, 0)),
                  cst(w1k), cst(w2k), cst(w1v), cst(w2v), cst(bias_c), cst(bid)],
        out_specs=[pl.BlockSpec((None, rows, DH_A), lambda b, pt: (b, 0, 0)),
                   pl.BlockSpec((None, nt, LANES), lambda b, pt: (b, 0, 0))],
        scratch_shapes=[pltpu.VMEM((2, 2, npages * PAGE_PITCH, DH_A), F32), pltpu.SemaphoreType.DMA((2,))],
    )
    return pl.pallas_call(
        kern,
        grid_spec=grid_spec,
        out_shape=[jax.ShapeDtypeStruct((nseq, rows, DH_A), F32), jax.ShapeDtypeStruct((nseq, nt, LANES), I32)],
        compiler_params=_cparams(("arbitrary",)),
        name="cmp_sample",
    )(page_table, cache3, q32, w1k, w2k, w1v, w2v, bias_c, bid)


def _slc_copy(cache_ref, pt_ref, sel_ref, buf_ref, sem_ref, seq, slot, i, nbk):
    blk = jnp.minimum(sel_ref[seq, i], nbk - 1)
    page = pt_ref[seq, blk // 2]
    r0 = pl.multiple_of((blk % 2) * BLK, BLK)
    return pltpu.make_async_copy(cache_ref.at[page, pl.ds(r0, BLK), pl.ds(2 * DH_A, 2 * DH_A)],
                                 buf_ref.at[slot, i], sem_ref.at[slot])


def _slc_sample_kernel(pt_ref, sel_ref, cache_ref, q_ref, rel_ref, new_ref, state_ref, wnew_ref, tbl_ref,
                       ocmp_ref, g0_ref, g1_ref, g2_ref, o_ref, buf_ref, sem_ref, *, nbk, nslot, nt, wb):
    b = pl.program_id(0)
    nseq = pl.num_programs(0)
    slot = b % 2
    ntot = nt * nslot

    def start_all(seq, sl):
        def body(i, c):
            @pl.when(sel_ref[seq, i] < nbk)
            def _():
                _slc_copy(cache_ref, pt_ref, sel_ref, buf_ref, sem_ref, seq, sl, i, nbk).start()
            return c
        lax.fori_loop(0, ntot, body, 0)

    @pl.when(b == 0)
    def _():
        start_all(0, 0)

    @pl.when(b + 1 < nseq)
    def _():
        start_all(b + 1, 1 - slot)

    def wait_body(i, c):
        @pl.when(sel_ref[b, i] < nbk)
        def _():
            _slc_copy(cache_ref, pt_ref, sel_ref, buf_ref, sem_ref, b, slot, i, nbk).wait()

        @pl.when(sel_ref[b, i] >= nbk)
        def _():
            buf_ref[slot, i] = new_ref[...]
        return c
    lax.fori_loop(0, ntot, wait_body, 0)

    state = state_ref[...]
    kst = state[:, 0:DH_A].astype(BF16)
    vst = state[:, DH_A:2 * DH_A].astype(BF16)
    wnew = wnew_ref[...]
    kwn = wnew[:, 0:DH_A].astype(BF16)
    vwn = wnew[:, DH_A:2 * DH_A].astype(BF16)
    st_iota = lax.broadcasted_iota(I32, (1, wb), 1)
    nw_iota = lax.broadcasted_iota(I32, (1, wnew.shape[0]), 1)
    for t in range(nt):
        q = q_ref[t * H_A:(t + 1) * H_A, :]
        kv = buf_ref[slot, t * nslot:(t + 1) * nslot].reshape(nslot * BLK, 2 * DH_A)
        rel = rel_ref[t:t + 1, :]
        lg = _dot_nt(q, kv[:, 0:DH_A].astype(BF16)) * SCALE_A + _t5_bias_rows(jnp.maximum(rel, 0), tbl_ref)
        ok = rel >= 0
        lg = jnp.where(ok, lg, NEG_INF)
        m = jnp.max(lg, axis=-1, keepdims=True)
        p = jnp.exp(lg - m) * ok.astype(F32)
        p = p / jnp.maximum(jnp.sum(p, axis=-1, keepdims=True), 1e-30)
        o_slc = _dot(p.astype(BF16), kv[:, DH_A:2 * DH_A].astype(BF16))

        rel_s = wb + t - st_iota
        rel_n = t - nw_iota
        ls = _dot_nt(q, kst) * SCALE_A + _t5_bias_rows(rel_s, tbl_ref)
        ln = _dot_nt(q, kwn) * SCALE_A + _t5_bias_rows(jnp.maximum(rel_n, 0), tbl_ref)
        ok_s = rel_s <= WINDOW
        ok_n = rel_n >= 0
        ls = jnp.where(ok_s, ls, NEG_INF)
        ln = jnp.where(ok_n, ln, NEG_INF)
        m = jnp.maximum(jnp.max(ls, axis=-1, keepdims=True), jnp.max(ln, axis=-1, keepdims=True))
        ps = jnp.exp(ls - m) * ok_s.astype(F32)
        pn = jnp.exp(ln - m) * ok_n.astype(F32)
        den = jnp.maximum(jnp.sum(ps, axis=-1, keepdims=True) + jnp.sum(pn, axis=-1, keepdims=True), 1e-30)
        o_win = _dot((ps / den).astype(BF16), vst) + _dot((pn / den).astype(BF16), vwn)
        r = slice(t * H_A, (t + 1) * H_A)
        o_ref[r, :] = g0_ref[r, :] * ocmp_ref[r, :] + g1_ref[r, :] * o_slc + g2_ref[r, :] * o_win


def _slc_sample(page_table, sel_flat, cache3, q32, rel, new_blk, state, wnew, tbl_t, ocmp, g0, g1, g2, nbk, nslot):
    nseq = page_table.shape[0]
    rows = q32.shape[1]
    nt = rows // H_A
    wb = state.shape[1]
    per = lambda a: pl.BlockSpec((None,) + a.shape[1:], lambda b, *_: (b,) + (0,) * (a.ndim - 1))
    kern = functools.partial(_slc_sample_kernel, nbk=nbk, nslot=nslot, nt=nt, wb=wb)
    grid_spec = pltpu.PrefetchScalarGridSpec(
        num_scalar_prefetch=2,
        grid=(nseq,),
        in_specs=[pl.BlockSpec(memory_space=pl.ANY), per(q32), per(rel), per(new_blk), per(state), per(wnew),
                  pl.BlockSpec(tbl_t.shape, lambda b, *_: (0, 0)), per(ocmp), per(g0), per(g1), per(g2)],
        out_specs=pl.BlockSpec((None, rows, DH_A), lambda b, *_: (b, 0, 0)),
        scratch_shapes=[pltpu.VMEM((2, nt * nslot, BLK, 2 * DH_A), F32), pltpu.SemaphoreType.DMA((2,))],
    )
    return pl.pallas_call(
        kern,
        grid_spec=grid_spec,
        out_shape=jax.ShapeDtypeStruct((nseq, rows, DH_A), F32),
        compiler_params=_cparams(("arbitrary",)),
        name="slc_sample",
    )(page_table, sel_flat, cache3, q32, rel, new_blk, state, wnew, tbl_t, ocmp, g0, g1, g2)


def _mla_chunk_copies(cache_ref, pt_ref, buf_ref, sem_ref, step, nchunk, cpages):
    seq = step // nchunk
    c = step % nchunk
    slot = step % MLA_RING

    def one(i):
        page = pt_ref[seq, c * cpages + i]
        return pltpu.make_async_copy(cache_ref.at[page],
                                     buf_ref.at[slot, pl.ds(pl.multiple_of(i * PAGE_SIZE, PAGE_SIZE), PAGE_SIZE), :],
                                     sem_ref.at[slot])
    return one


def _mla_sample_kernel(pt_ref, cache_ref, ql_ref, qr_ref, new_ref, o_ref, buf_ref, sem_ref, m_ref, l_ref, acc_ref,
                       *, nchunk, cpages, nt):
    s = pl.program_id(0)
    nstep = pl.num_programs(0)
    c = s % nchunk
    slot = s % MLA_RING
    ahead = MLA_RING - 1

    def start_chunk(step):
        one = _mla_chunk_copies(cache_ref, pt_ref, buf_ref, sem_ref, step, nchunk, cpages)

        def body(i, cc):
            one(i).start()
            return cc
        lax.fori_loop(0, cpages, body, 0)

    @pl.when(s == 0)
    def _():
        for d in range(ahead):
            @pl.when(d < nstep)
            def _():
                start_chunk(d)

    @pl.when(s + ahead < nstep)
    def _():
        start_chunk(s + ahead)

    ql = ql_ref[...]
    qr = qr_ref[...]

    @pl.when(c == 0)
    def _():
        new = new_ref[...]
        sn = (_dot_nt(ql, new[:, 0:KV_LORA].astype(BF16)) + _dot_nt(qr, new[:, KV_LORA:MLA_ROW].astype(BF16))) * SCALE_B
        rows = sn.shape[0]
        t_of_row = lax.broadcasted_iota(I32, (rows, 1), 0) // H_B
        j = lax.broadcasted_iota(I32, (1, new.shape[0]), 1)
        ok = j <= t_of_row
        sn = jnp.where(ok, sn, NEG_INF)
        m = jnp.max(sn, axis=-1, keepdims=True)
        p = jnp.exp(sn - m) * ok.astype(F32)
        m_ref[...] = m
        l_ref[...] = jnp.sum(p, axis=-1, keepdims=True)
        acc_ref[...] = _dot(p.astype(BF16), new[:, 0:KV_LORA].astype(BF16))

    one = _mla_chunk_copies(cache_ref, pt_ref, buf_ref, sem_ref, s, nchunk, cpages)

    def wait_body(i, cc):
        one(i).wait()
        return cc
    lax.fori_loop(0, cpages, wait_body, 0)

    kv = buf_ref[slot]
    lat = kv[:, 0:KV_LORA].astype(BF16)
    kr = kv[:, KV_LORA:MLA_ROW].astype(BF16)
    sc = (_dot_nt(ql, lat) + _dot_nt(qr, kr)) * SCALE_B
    m_old = m_ref[...]
    m_new = jnp.maximum(m_old, jnp.max(sc, axis=-1, keepdims=True))
    alpha = jnp.exp(m_old - m_new)
    p = jnp.exp(sc - m_new)
    l_new = alpha * l_ref[...] + jnp.sum(p, axis=-1, keepdims=True)
    acc = alpha * acc_ref[...] + _dot(p.astype(BF16), lat)
    m_ref[...] = m_new
    l_ref[...] = l_new
    acc_ref[...] = acc

    @pl.when(c == nchunk - 1)
    def _():
        o_ref[...] = acc / jnp.maximum(l_new, 1e-30)


def _mla_sample(page_table, cache_mla, qlat, qrope, new_rows):
    nseq, npages = page_table.shape
    rows = qlat.shape[1]
    nt = rows // H_B
    cpages = min(MLA_CHUNK_PAGES, npages)
    nchunk = npages // cpages
    kern = functools.partial(_mla_sample_kernel, nchunk=nchunk, cpages=cpages, nt=nt)
    per = lambda a: pl.BlockSpec((None,) + a.shape[1:], lambda s, pt: (s // nchunk,) + (0,) * (a.ndim - 1))
    grid_spec = pltpu.PrefetchScalarGridSpec(
        num_scalar_prefetch=1,
        grid=(nseq * nchunk,),
        in_specs=[pl.BlockSpec(memory_space=pl.ANY), per(qlat), per(qrope), per(new_rows)],
        out_specs=pl.BlockSpec((None, rows, KV_LORA), lambda s, pt: (s // nchunk, 0, 0)),
        scratch_shapes=[pltpu.VMEM((MLA_RING, cpages * PAGE_SIZE, MLA_ROW), F32), pltpu.SemaphoreType.DMA((MLA_RING,)),
                        pltpu.VMEM((rows, 1), F32), pltpu.VMEM((rows, 1), F32), pltpu.VMEM((rows, KV_LORA), F32)],
    )
    return pl.pallas_call(
        kern,
        grid_spec=grid_spec,
        out_shape=jax.ShapeDtypeStruct((nseq, rows, KV_LORA), F32),
        compiler_params=_cparams(("arbitrary",)),
        name="mla_sample",
    )(page_table, cache_mla, qlat, qrope, new_rows)


def _prep_w_in(w_in):
    d = w_in.shape[0]
    sizes = (H_A * DH_A, 4 * DH_A, 2 * DH_A, 3 * H_A, Q_LORA, KV_LORA, D_ROPE)
    offs = np.concatenate([[0], np.cumsum(sizes)])
    seg = lambda i: w_in[:, int(offs[i]):int(offs[i + 1])]
    zq, zkv, zwin, zg, zcq, zckv, zkr = (seg(i) for i in range(7))
    half = D_ROPE // 2
    zrot = jnp.concatenate([-zkr[:, half:], zkr[:, :half]], axis=1)
    pad = lambda a: jnp.pad(a, ((0, 0), (0, LANES - a.shape[1])))
    w_ext = jnp.concatenate([zq, zkv, zwin, zcq, zckv, pad(zkr), pad(zrot), pad(zg)], axis=1)
    assert w_ext.shape == (d, C_END)
    return w_ext.astype(BF16)


def _prep_w_uq(w_uq):
    r = w_uq.shape[0]
    w = w_uq.reshape(r, H_B, D_NOPE + D_ROPE)
    nope = w[:, :, :D_NOPE].reshape(r, H_B * D_NOPE)
    rope = w[:, :, D_NOPE:]
    half = D_ROPE // 2
    rot = jnp.concatenate([-rope[:, :, half:], rope[:, :, :half]], axis=2)
    padh = lambda a: jnp.pad(a, ((0, 0), (0, 0), (0, LANES - D_ROPE))).reshape(r, H_B * LANES)
    return jnp.concatenate([nope, padh(rope), padh(rot)], axis=1).astype(BF16)


def _rope_tables(pos):
    half = D_ROPE // 2
    inv = ROPE_THETA ** (-jnp.arange(half, dtype=F32) / half)
    ang = pos.astype(F32)[:, None] * inv[None, :]
    cos = jnp.cos(ang)
    sin = jnp.sin(ang)
    z = jnp.zeros((pos.shape[0], LANES - D_ROPE), F32)
    return jnp.concatenate([cos, cos, z], axis=1), jnp.concatenate([sin, sin, z], axis=1)


def _bias_from_rel(rel_np, rel_table):
    return rel_table[jnp.asarray(_t5_bucket_np(rel_np))]


def kernel(x_prompt, x_sample, cache_nsa_kv, cache_mla, state_win_kv, page_table, c_prompt, c_sample, rel_table,
           g_attn, g_ffn, g_final, w_ada, b_ada, w_in, g_qnorm, w_uq, g_kvnorm, w_uk, w_uv, w_ck1, w_ck2, w_cv1,
           w_cv2, w_out, w_gate, w_up, w_down):
    n_p, seq, d = x_prompt.shape
    n_s, t_s, _ = x_sample.shape
    npages = page_table.shape[1]
    past = npages * PAGE_SIZE
    wb = state_win_kv.shape[1]
    n_pool = cache_nsa_kv.shape[0]
    assert seq % K_TILE == 0 and seq >= WINDOW and past % PAGE_SIZE == 0 and wb == WINDOW and t_s <= 8

    w_ext = _prep_w_in(w_in)
    wuq_ext = _prep_w_uq(w_uq)
    wkv = jnp.concatenate([w_uk.reshape(KV_LORA, H_B * D_NOPE), w_uv.reshape(KV_LORA, H_B * D_V)], axis=1).astype(BF16)
    w_out_b = w_out.astype(BF16)
    wg_b, wu_b, wd_b = w_gate.astype(BF16), w_up.astype(BF16), w_down.astype(BF16)
    w1k_b, w2k_b, w1v_b, w2v_b = (w.astype(BF16) for w in (w_ck1, w_ck2, w_cv1, w_cv2))

    n_c = n_p + n_s
    c_all = jnp.pad(jnp.concatenate([c_prompt, c_sample], axis=0), ((0, (-n_c) % 8), (0, 0)))
    mod = _modulation(c_all, w_ada, b_ada)
    mod_p = mod[:n_p].reshape(n_p, 6, 1, d)
    mod_s = jnp.repeat(mod[n_p:n_c].reshape(n_s, 6, d), t_s, axis=0).transpose(1, 0, 2)[:, None]
    sh_a, sc_a, gt_a, sh_f, sc_f, gt_f = (mod_p[:, i] for i in range(6))
    sh_a_s, sc_a_s, gt_a_s, sh_f_s, sc_f_s, gt_f_s = (mod_s[i] for i in range(6))

    cos_p, sin_p = _rope_tables(jnp.arange(seq, dtype=I32))
    pos_s = past + jnp.tile(jnp.arange(t_s, dtype=I32), n_s)
    cos_s, sin_s = _rope_tables(pos_s)
    xp2 = x_prompt.reshape(n_p * seq, d)
    xs2 = x_sample.reshape(n_s * t_s, d)
    (qn_p, nsa_p, win_p, gates_p, mla_p, qnope_p, qrope_p, krp_p, knope_p, v_p) = _inproj(
        xp2, sc_a, sh_a, g_attn, w_ext, g_qnorm, wuq_ext, g_kvnorm, wkv, cos_p, sin_p, seq)
    (qn_s, nsa_s, win_s, gates_s, mla_s, qnope_s, qrope_s, _krp_s, _kn_s, _v_s) = _inproj(
        xs2, sc_a_s, sh_a_s, g_attn, w_ext, g_qnorm, wuq_ext, g_kvnorm, wkv, cos_s, sin_s, t_s)

    nb = seq // BLK
    nsa_p4 = nsa_p.reshape(n_p, seq, 4, DH_A)
    kb = nsa_p4[:, :, 0].reshape(n_p * nb, BLK * DH_A)
    vb = nsa_p4[:, :, 1].reshape(n_p * nb, BLK * DH_A)
    ck_p, cv_p = _compress_prompt(kb, vb, w1k_b, w2k_b, w1v_b, w2v_b)
    ck_p = ck_p.reshape(n_p, nb, DH_A).astype(BF16)
    cvt_p = cv_p.reshape(n_p, nb, DH_A).transpose(0, 2, 1).astype(BF16)

    tq = np.arange(seq)
    bias_ct = _bias_from_rel(tq[None, :] - (np.arange(nb) * BLK + BLK - 1)[:, None], rel_table)
    bias_ct = bias_ct.transpose(2, 0, 1)
    band_t = _bias_from_rel(np.arange(Q_TILE)[None, :] - (np.arange(2 * Q_TILE) - Q_TILE)[:, None], rel_table)
    band_t = band_t.transpose(2, 0, 1)
    c31 = jnp.broadcast_to(rel_table[N_BUCKETS - 1][:, None, None], (H_A, 1, Q_TILE))
    gates_t = gates_p[:, :3 * H_A].reshape(n_p, seq, 3 * H_A).transpose(0, 2, 1)

    ks = nsa_p4[:, :, 2].astype(BF16)
    vs = nsa_p4[:, :, 3].astype(BF16)
    win_p4 = win_p.reshape(n_p, seq, 2, DH_A)
    kw = win_p4[:, :, 0].astype(BF16)
    vw = win_p4[:, :, 1].astype(BF16)
    ks_p = jnp.pad(ks, ((0, 0), (Q_TILE, 0), (0, 0)))
    vst_p = jnp.pad(vs, ((0, 0), (Q_TILE, 0), (0, 0))).transpose(0, 2, 1)
    kw_p = jnp.pad(kw, ((0, 0), (WINDOW, 0), (0, 0)))
    vwt_p = jnp.pad(vw, ((0, 0), (WINDOW, 0), (0, 0))).transpose(0, 2, 1)
    o_nsa_p = _nsa_prompt(qn_p, gates_t, ck_p, cvt_p, bias_ct, band_t, c31, ks_p, vst_p, kw_p, vwt_p, n_p, seq)

    o_mla_p = _mla_prompt(qnope_p, qrope_p, knope_p.reshape(n_p, seq, H_B * D_NOPE), krp_p.reshape(n_p, seq, LANES),
                          v_p.reshape(n_p, seq, H_B * D_V).transpose(0, 2, 1), n_p, seq)

    rows_s = t_s * H_A
    nbk = past // BLK
    nbs = -(-(past + t_s) // BLK)
    assert nbs == nbk + 1
    k_sel = min(N_SEL, nbs)
    cache3 = cache_nsa_kv.reshape(n_pool, PAGE_SIZE, 4 * DH_A)
    q32 = qn_s.reshape(n_s, rows_s, DH_A)

    order = np.concatenate([np.arange(0, nbk, 2), np.arange(1, nbk, 2)])
    width = -(-(nbk + 1) // LANES) * LANES
    bid_np = np.full((1, width), -1, np.float32)
    bid_np[0, :nbk] = order
    bid_np[0, nbk] = nbk
    qpos_np = past + np.arange(t_s)
    rel_c = qpos_np[:, None] - (order * BLK + BLK - 1)[None, :]
    bias_c = _bias_from_rel(rel_c, rel_table).transpose(0, 2, 1).reshape(rows_s, nbk)
    w1k3 = w1k_b.reshape(BLK, DH_A, DH_A)
    w1v3 = w1v_b.reshape(BLK, DH_A, DH_A)
    ocmp_s, sel_pad = _cmp_sample(page_table, cache3, q32, w1k3, w2k_b, w1v3, w2v_b, bias_c, jnp.asarray(bid_np), nbs)
    sel = sel_pad[:, :, :k_sel]

    tok = sel[..., None] * BLK + jnp.arange(BLK, dtype=I32)
    rel_slc = (jnp.asarray(qpos_np, I32)[None, :, None, None] - tok).reshape(n_s, t_s, k_sel * BLK)
    nsa_s4 = nsa_s.reshape(n_s, t_s, 4, DH_A)
    new_blk = jnp.pad(nsa_s4[:, :, 2:4].reshape(n_s, t_s, 2 * DH_A), ((0, 0), (0, BLK - t_s), (0, 0)))
    win_s3 = win_s.reshape(n_s, t_s, 2 * DH_A)
    wnew = jnp.pad(win_s3, ((0, 0), (0, 8 - t_s), (0, 0)))
    gs = gates_s[:, :3 * H_A].reshape(n_s, rows_s, 3)
    g0, g1, g2 = (jnp.broadcast_to(gs[:, :, i:i + 1], (n_s, rows_s, DH_A)) for i in range(3))
    o_nsa_s = _slc_sample(page_table, sel.reshape(n_s, t_s * k_sel), cache3, q32, rel_slc, new_blk,
                          state_win_kv.reshape(n_s, wb, 2 * DH_A), wnew, rel_table.T, ocmp_s, g0, g1, g2, nbk, k_sel)

    wuk_t = w_uk.transpose(1, 2, 0).astype(BF16)
    qlat = _headproj(qnope_s, wuk_t).reshape(n_s, rows_s, KV_LORA).astype(BF16)
    qrope_s3 = qrope_s.reshape(n_s, t_s, H_B, LANES)[..., :D_ROPE].reshape(n_s, rows_s, D_ROPE)
    mla_new = jnp.pad(mla_s.reshape(n_s, t_s, MLA_ROW), ((0, 0), (0, 8 - t_s), (0, 0)))
    o_lat = _mla_sample(page_table, cache_mla, qlat, qrope_s3, mla_new)
    wuv_h = w_uv.transpose(1, 0, 2).astype(BF16)
    o_mla_s = _headproj(o_lat.reshape(n_s * t_s, H_B * KV_LORA), wuv_h)

    x1_p, h2_p = _outproj(o_nsa_p, o_mla_p, w_out_b, xp2, gt_a, sc_f, sh_f, g_ffn, seq)
    y_p = _ffn(h2_p, wg_b, wu_b, wd_b, x1_p, gt_f, g_final, seq)
    oa_s = o_nsa_s.reshape(n_s * t_s, H_A * DH_A).astype(BF16)
    x1_s, h2_s = _outproj(oa_s, o_mla_s.astype(BF16), w_out_b, xs2, gt_a_s, sc_f_s, sh_f_s, g_ffn, t_s)
    y_s = _ffn(h2_s, wg_b, wu_b, wd_b, x1_s, gt_f_s, g_final, t_s)

    win_all = jnp.concatenate([state_win_kv, win_s.reshape(n_s, t_s, 2, DH_A)], axis=1)
    return (y_p.reshape(n_p, seq, d), y_s.reshape(n_s, t_s, d),
            nsa_p4, mla_p.reshape(n_p, seq, MLA_ROW), win_p4[:, -min(WINDOW, seq):],
            nsa_s4, mla_s.reshape(n_s, t_s, MLA_ROW), win_all[:, -min(WINDOW, wb + t_s):])
```

```python
import functools
import math

import numpy as np
import jax
import jax.numpy as jnp
from jax import lax
from jax.experimental import pallas as pl
from jax.experimental.pallas import tpu as pltpu

F32 = jnp.float32
BF16 = jnp.bfloat16
I32 = jnp.int32

H_A = 8
DH_A = 128
BLK = 64
N_SEL = 16
WINDOW = 512
H_B = 8
D_NOPE = 128
D_ROPE = 64
D_V = 128
Q_LORA = 512
KV_LORA = 512
MLA_ROW = KV_LORA + D_ROPE
PAGE_SIZE = 128
ROPE_THETA = 10000.0
N_BUCKETS = 32
T5_MAX_DIST = 128
LOG2E = math.log2(math.e)
QSCALE_A = DH_A ** -0.5 * LOG2E
QSCALE_B = (D_NOPE + D_ROPE) ** -0.5 * LOG2E
EPS = 1e-6
NEG_INF = -1e30
FORCED_SCORE = 1e4

Q_TILE = 128
K_TILE = 512
LANES = 128
V7X_VMEM_LIMIT = 56 * 1024 * 1024
NSA_ROWS = 4
PAGE_ROWS = PAGE_SIZE * NSA_ROWS
BLK_ROWS = BLK * NSA_ROWS
CHUNK_PAGES = 16
CMP_RING = 4
MLA_RING = 3


def _t5_bucket_np(n):
    n = np.maximum(np.asarray(n, np.int64), 0)
    exact = N_BUCKETS // 2
    nf = np.maximum(n, 1).astype(np.float32)
    far = exact + (np.log(nf / np.float32(exact)) / np.float32(math.log(T5_MAX_DIST / exact))
                   * np.float32(N_BUCKETS - exact)).astype(np.int32)
    return np.where(n < exact, n, np.minimum(far, N_BUCKETS - 1)).astype(np.int32)


_BUCKET_START = [int(np.argmax(_t5_bucket_np(np.arange(4 * T5_MAX_DIST)) >= b)) for b in range(N_BUCKETS)]


def _dot(a, b):
    return jnp.dot(a, b, preferred_element_type=F32)


def _dot_nt(a, b):
    return lax.dot_general(a, b, (((1,), (1,)), ((), ())), preferred_element_type=F32)


def _rms(x, g):
    return x * lax.rsqrt(jnp.mean(x * x, axis=-1, keepdims=True) + EPS) * g


def _cparams(sem, vmem=V7X_VMEM_LIMIT):
    return pltpu.CompilerParams(dimension_semantics=sem, vmem_limit_bytes=vmem)


def _const_spec(shape):
    nd = len(shape)
    return pl.BlockSpec(shape, lambda *_: (0,) * nd, pipeline_mode=pl.Buffered(1))


def _smem_spec():
    return pl.BlockSpec(memory_space=pltpu.SMEM)


def _bias_from_smem(n, tbl_ref, h, offset=0.0):
    bias = jnp.full(n.shape, 0.0, F32) + (tbl_ref[N_BUCKETS - 1, h] - offset) * LOG2E
    for b in range(N_BUCKETS - 2, -1, -1):
        bias = jnp.where(n < _BUCKET_START[b + 1], (tbl_ref[b, h] - offset) * LOG2E, bias)
    return bias


def _bias_from_cols(n, tbl_ref):
    bias = jnp.broadcast_to(tbl_ref[:, N_BUCKETS - 1:N_BUCKETS], (tbl_ref.shape[0], n.shape[1]))
    for b in range(N_BUCKETS - 2, -1, -1):
        bias = jnp.where(n < _BUCKET_START[b + 1], tbl_ref[:, b:b + 1], bias)
    return bias * LOG2E


def _mod_kernel(c_ref, w_ref, b_ref, o_ref):
    c = c_ref[...]
    s = (c * jax.nn.sigmoid(c)).astype(BF16)
    o_ref[...] = _dot(s, w_ref[...].astype(BF16)) + b_ref[...]


def _modulation(c_all, w_ada, b_ada):
    m, d = c_all.shape
    n = w_ada.shape[1]
    tn = 512
    return pl.pallas_call(
        _mod_kernel,
        grid=(n // tn,),
        in_specs=[pl.BlockSpec((m, d), lambda j: (0, 0)),
                  pl.BlockSpec((d, tn), lambda j: (0, j)),
                  pl.BlockSpec((1, tn), lambda j: (0, j))],
        out_specs=pl.BlockSpec((m, tn), lambda j: (0, j)),
        out_shape=jax.ShapeDtypeStruct((m, n), F32),
        compiler_params=_cparams(("arbitrary",)),
        name="adaln_mod",
    )(c_all, w_ada, b_ada.reshape(1, n))


C_QN = 0
C_KV = C_QN + H_A * DH_A
C_WIN = C_KV + 4 * DH_A
C_CQ = C_WIN + 2 * DH_A
C_CKV = C_CQ + Q_LORA
C_KR = C_CKV + KV_LORA
C_KRR = C_KR + LANES
C_G = C_KRR + LANES
C_END = C_G + LANES


def _inproj_kernel(x_ref, sc_ref, sh_ref, g_ref, w_ref, gq_ref, wuq_ref, gkv_ref, wkv_ref, cos_ref, sin_ref,
                   qn_ref, nsa_ref, win_ref, gates_ref, mla_ref, qnope_ref, qrope_ref, krp_ref, knope_ref, v_ref):
    x = x_ref[...]
    h = _rms(x, g_ref[...]) * (1.0 + sc_ref[...]) + sh_ref[...]
    z = _dot(h.astype(BF16), w_ref[...])
    qn_ref[...] = (z[:, C_QN:C_KV] * QSCALE_A).astype(BF16)
    nsa_ref[...] = z[:, C_KV:C_WIN]
    win_ref[...] = z[:, C_WIN:C_CQ]
    gates_ref[...] = jax.nn.sigmoid(z[:, C_G:C_END])
    cos = cos_ref[...]
    sin = sin_ref[...]
    kvn = _rms(z[:, C_CKV:C_KR], gkv_ref[...])
    krp = z[:, C_KR:C_KRR] * cos + z[:, C_KRR:C_G] * sin
    mla_ref[:, 0:KV_LORA] = kvn
    mla_ref[:, KV_LORA:MLA_ROW] = krp[:, 0:D_ROPE]
    krp_ref[...] = krp.astype(BF16)
    cqn = _rms(z[:, C_CQ:C_CKV], gq_ref[...])
    q = _dot(cqn.astype(BF16), wuq_ref[...])
    nq = H_B * D_NOPE
    qnope_ref[...] = (q[:, 0:nq] * QSCALE_B).astype(BF16)
    for hh in range(H_B):
        a = q[:, nq + hh * LANES: nq + (hh + 1) * LANES]
        b = q[:, 2 * nq + hh * LANES: 2 * nq + (hh + 1) * LANES]
        qrope_ref[:, hh * LANES:(hh + 1) * LANES] = ((a * cos + b * sin) * QSCALE_B).astype(BF16)
    kv = _dot(kvn.astype(BF16), wkv_ref[...])
    knope_ref[...] = kv[:, 0:nq].astype(BF16)
    v_ref[...] = kv[:, nq:2 * nq].astype(BF16)


def _mod_spec(arr, tm, rows_per_group):
    d = arr.shape[-1]
    if arr.shape[1] != 1:
        return pl.BlockSpec((None, tm, d), lambda i, *_: (0, i, 0))
    tps = rows_per_group // tm
    return pl.BlockSpec((None, 1, d), lambda i, *_: (i // tps, 0, 0))


def _inproj(x2d, sc, sh, g_attn, w_ext, g_q, wuq_ext, g_kv, wkv, cos_t, sin_t, rows_per_group):
    rows, d = x2d.shape
    tm = min(512, rows)
    nt = rows // tm
    mod_spec = _mod_spec(sc, tm, rows_per_group)
    ptiles = cos_t.shape[0] // tm
    pos_spec = pl.BlockSpec((tm, LANES), lambda i: (i % ptiles, 0))
    row = lambda n: pl.BlockSpec((tm, n), lambda i: (i, 0))
    nq = H_B * D_NOPE
    outs = [(H_A * DH_A, BF16), (4 * DH_A, F32), (2 * DH_A, F32), (LANES, F32), (MLA_ROW, F32),
            (nq, BF16), (H_B * LANES, BF16), (LANES, BF16), (nq, BF16), (nq, BF16)]
    return pl.pallas_call(
        _inproj_kernel,
        grid=(nt,),
        in_specs=[row(d), mod_spec, mod_spec, _const_spec((1, d)), _const_spec(w_ext.shape),
                  _const_spec((1, Q_LORA)), _const_spec(wuq_ext.shape), _const_spec((1, KV_LORA)),
                  _const_spec(wkv.shape), pos_spec, pos_spec],
        out_specs=[row(n) for n, _ in outs],
        out_shape=[jax.ShapeDtypeStruct((rows, n), dt) for n, dt in outs],
        compiler_params=_cparams(("arbitrary",)),
        name="inproj",
    )(x2d, sc, sh, g_attn.reshape(1, d), w_ext, g_q.reshape(1, Q_LORA), wuq_ext, g_kv.reshape(1, KV_LORA),
      wkv, cos_t, sin_t)


def _compress_kernel(kb_ref, vb_ref, w1k_ref, w2k_ref, w1v_ref, w2v_ref, ck_ref, cv_ref):
    hk = jax.nn.gelu(_dot(kb_ref[...].astype(BF16), w1k_ref[...]))
    ck_ref[...] = _dot(hk.astype(BF16), w2k_ref[...])
    hv = jax.nn.gelu(_dot(vb_ref[...].astype(BF16), w1v_ref[...]))
    cv_ref[...] = _dot(hv.astype(BF16), w2v_ref[...])


def _compress_prompt(kb, vb, w1k, w2k, w1v, w2v):
    nblk = kb.shape[0]
    full = lambda a: pl.BlockSpec(a.shape, lambda i: (0,) * a.ndim)
    args = (kb, vb, w1k, w2k, w1v, w2v)
    return pl.pallas_call(
        _compress_kernel,
        grid=(1,),
        in_specs=[full(a) for a in args],
        out_specs=[pl.BlockSpec((nblk, DH_A), lambda i: (0, 0))] * 2,
        out_shape=[jax.ShapeDtypeStruct((nblk, DH_A), F32)] * 2,
        compiler_params=_cparams(("arbitrary",)),
        name="compress_prompt",
    )(*args)


def _flash_reset(m_ref, l_ref, a_ref):
    m_ref[...] = jnp.full(m_ref.shape, NEG_INF, F32)
    l_ref[...] = jnp.zeros(l_ref.shape, F32)
    a_ref[...] = jnp.zeros(a_ref.shape, F32)


def _flash_tile(m_ref, l_ref, a_ref, logits, v_of):
    probs = []
    for h, s in enumerate(logits):
        m = m_ref[h]
        m_new = jnp.maximum(m, jnp.max(s, axis=0, keepdims=True))
        alpha = jnp.exp2(m - m_new)
        p = jnp.exp2(s - m_new)
        l_ref[h] = alpha * l_ref[h] + jnp.sum(p, axis=0, keepdims=True)
        m_ref[h] = m_new
        probs.append((alpha, p.astype(BF16)))
    for h, (alpha, p) in enumerate(probs):
        a_ref[h] = alpha * a_ref[h] + _dot(v_of(h), p)


def _nsa_prompt_kernel(tbl_ref, q_ref, gt_ref, ck_ref, cvt_ref, ks_ref, vst_ref, kw_ref, vwt_ref, o_ref,
                       out_ref, m_ref, l_ref, a_ref, band_ref, cband_ref, score_ref, self_ref, selr_ref, madd_ref,
                       nearm_ref, *, nb, k_sel):
    qb = pl.program_id(1)
    qs = qb * Q_TILE
    t_iota = lax.broadcasted_iota(I32, (1, Q_TILE), 1)
    tpos = qs + t_iota
    b_iota = lax.broadcasted_iota(I32, (nb, 1), 0)
    kp_iota = lax.broadcasted_iota(I32, (2 * Q_TILE, 1), 0)

    @pl.when((pl.program_id(0) == 0) & (qb == 0))
    def _():
        n_band = jnp.maximum(t_iota - kp_iota + Q_TILE, 0)
        j_iota = lax.broadcasted_iota(I32, (cband_ref.shape[1], 1), 0)
        n_cb = jnp.maximum(t_iota - (BLK - 1) - BLK * (j_iota - 2), 0)
        for h in range(H_A):
            band_ref[h] = _bias_from_smem(n_band, tbl_ref, h, offset=tbl_ref[N_BUCKETS - 1, h])
            cband_ref[h] = _bias_from_smem(n_cb, tbl_ref, h)

    blk_end = b_iota * BLK + (BLK - 1)
    valid_c = blk_end <= tpos
    valid_cf = valid_c.astype(F32)
    j_rel = b_iota - 2 * qb + 2
    ck = ck_ref[...]
    cvt = cvt_ref[...]

    def cmp_bias(h):
        bias = jnp.full((nb, Q_TILE), 0.0, F32) + tbl_ref[N_BUCKETS - 1, h] * LOG2E
        for j in range(4):
            bias = jnp.where(j_rel == j, cband_ref[h, j:j + 1, :], bias)
        return bias

    imp = jnp.zeros((nb, Q_TILE), F32)
    lgs = [_dot_nt(ck, q_ref[:, h * DH_A:(h + 1) * DH_A]) for h in range(H_A)]
    ps = []
    for h in range(H_A):
        lg = jnp.where(valid_c, lgs[h] + cmp_bias(h), NEG_INF)
        m = jnp.max(lg, axis=0, keepdims=True)
        p = jnp.exp2(lg - m) * valid_cf
        p = p / jnp.maximum(jnp.sum(p, axis=0, keepdims=True), 1e-30)
        imp = imp + p
        ps.append(p.astype(BF16))
    for h in range(H_A):
        out_ref[h] = gt_ref[3 * h:3 * h + 1, :] * _dot(cvt, ps[h])

    cur = tpos // BLK
    forced = (b_iota == 0) | (b_iota == cur) | (b_iota == cur - 1)
    score = jnp.where(forced, FORCED_SCORE, jnp.where(b_iota <= cur, imp, -FORCED_SCORE))
    score_ref[...] = score

    def rank_body(bp, rank):
        row = score_ref[pl.ds(bp, 1), :]
        tie = jnp.where(bp < b_iota, 1, 0)
        return rank + jnp.where(row > score, 1, jnp.where(row == score, tie, 0))

    rank = lax.fori_loop(0, nb, rank_body, jnp.zeros((nb, Q_TILE), I32))
    sel = rank < k_sel
    selr_ref[...] = sel.astype(F32)
    self_ref[...] = jnp.where(sel & (b_iota < 2 * qb - 2), 0.0, NEG_INF)
    nfar = (jnp.maximum(qb - 1, 0) * Q_TILE + (K_TILE - 1)) // K_TILE
    blocks_per_tile = K_TILE // BLK

    def mask_body(j, c):
        for i in range(blocks_per_tile):
            row = self_ref[pl.ds(j * blocks_per_tile + i, 1), :]
            madd_ref[pl.ds(pl.multiple_of(j * K_TILE + i * BLK, BLK), BLK), :] = jnp.broadcast_to(row, (BLK, Q_TILE))
        return c

    lax.fori_loop(0, nfar, mask_body, 0)

    causal = (kp_iota - Q_TILE) <= t_iota
    in_seq = (kp_iota + qs) >= Q_TILE
    for i in range(4):
        bi = 2 * qb - 2 + i
        row = selr_ref[pl.ds(jnp.maximum(bi, 0), 1), :]
        ok = (row > 0.5) & causal[i * BLK:(i + 1) * BLK] & (bi >= 0)
        nearm_ref[i * BLK:(i + 1) * BLK, :] = jnp.where(ok, 0.0, NEG_INF)
    win_near = jnp.where(causal & in_seq, 0.0, NEG_INF)
    nwf = WINDOW - Q_TILE
    wf_iota = lax.broadcasted_iota(I32, (nwf, 1), 0)
    win_far = jnp.where((wf_iota >= t_iota) & ((wf_iota + qs) >= WINDOW), 0.0, NEG_INF)

    _flash_reset(m_ref, l_ref, a_ref)
    heads = range(H_A)
    q_of = lambda h: q_ref[:, h * DH_A:(h + 1) * DH_A]

    def far_body(j, c):
        r0 = pl.multiple_of(Q_TILE + j * K_TILE, Q_TILE)
        kt = ks_ref[pl.ds(r0, K_TILE), :]
        vt = vst_ref[:, pl.ds(r0, K_TILE)]
        mt = madd_ref[pl.ds(pl.multiple_of(j * K_TILE, K_TILE), K_TILE), :]
        _flash_tile(m_ref, l_ref, a_ref, [_dot_nt(kt, q_of(h)) + mt for h in heads], lambda h: vt)
        return c

    lax.fori_loop(0, nfar, far_body, 0)
    near0 = pl.multiple_of(qs, Q_TILE)
    kt = ks_ref[pl.ds(near0, 2 * Q_TILE), :]
    vt = vst_ref[:, pl.ds(near0, 2 * Q_TILE)]
    nearm = nearm_ref[...]
    _flash_tile(m_ref, l_ref, a_ref, [_dot_nt(kt, q_of(h)) + (band_ref[h] + nearm) for h in heads], lambda h: vt)
    for h in heads:
        out_ref[h] += gt_ref[3 * h + 1:3 * h + 2, :] * (a_ref[h] / jnp.maximum(l_ref[h], 1e-30))

    _flash_reset(m_ref, l_ref, a_ref)
    kt = kw_ref[pl.ds(near0, nwf), :]
    vt = vwt_ref[:, pl.ds(near0, nwf)]
    _flash_tile(m_ref, l_ref, a_ref, [_dot_nt(kt, q_of(h)) + win_far for h in heads], lambda h: vt)
    wn0 = pl.multiple_of(qs + nwf, Q_TILE)
    kt2 = kw_ref[pl.ds(wn0, 2 * Q_TILE), :]
    vt2 = vwt_ref[:, pl.ds(wn0, 2 * Q_TILE)]
    _flash_tile(m_ref, l_ref, a_ref, [_dot_nt(kt2, q_of(h)) + (band_ref[h] + win_near) for h in heads], lambda h: vt2)
    for h in heads:
        tot = out_ref[h] + gt_ref[3 * h + 2:3 * h + 3, :] * (a_ref[h] / jnp.maximum(l_ref[h], 1e-30))
        o_ref[:, h * DH_A:(h + 1) * DH_A] = tot.T.astype(BF16)


def _nsa_prompt(rel_table, qn, gates_t, ck, cvt, ks_p, vst_p, kw_p, vwt_p, n_seq, seq):
    nb = seq // BLK
    nq = seq // Q_TILE
    k_sel = min(N_SEL, nb)
    per_seq = lambda a: pl.BlockSpec((None,) + a.shape[1:], lambda n, i: (n,) + (0,) * (a.ndim - 1),
                                     pipeline_mode=pl.Buffered(1))
    kern = functools.partial(_nsa_prompt_kernel, nb=nb, k_sel=k_sel)
    stat = pltpu.VMEM((H_A, 1, Q_TILE), F32)
    return pl.pallas_call(
        kern,
        grid=(n_seq, nq),
        in_specs=[_smem_spec(),
                  pl.BlockSpec((Q_TILE, H_A * DH_A), lambda n, i: (n * nq + i, 0)),
                  pl.BlockSpec((None, 3 * H_A, Q_TILE), lambda n, i: (n, 0, i)),
                  per_seq(ck), per_seq(cvt), per_seq(ks_p), per_seq(vst_p), per_seq(kw_p), per_seq(vwt_p)],
        out_specs=pl.BlockSpec((Q_TILE, H_A * DH_A), lambda n, i: (n * nq + i, 0)),
        out_shape=jax.ShapeDtypeStruct((n_seq * seq, H_A * DH_A), BF16),
        scratch_shapes=[pltpu.VMEM((H_A, DH_A, Q_TILE), F32), stat, stat, pltpu.VMEM((H_A, DH_A, Q_TILE), F32),
                        pltpu.VMEM((H_A, 2 * Q_TILE, Q_TILE), F32), pltpu.VMEM((H_A, 8, Q_TILE), F32),
                        pltpu.VMEM((nb, Q_TILE), F32), pltpu.VMEM((nb, Q_TILE), F32), pltpu.VMEM((nb, Q_TILE), F32),
                        pltpu.VMEM((seq, Q_TILE), F32), pltpu.VMEM((2 * Q_TILE, Q_TILE), F32)],
        compiler_params=_cparams(("arbitrary", "arbitrary")),
        name="nsa_prompt",
    )(rel_table, qn, gates_t, ck, cvt, ks_p, vst_p, kw_p, vwt_p)


def _mla_prompt_kernel(qn_ref, qr_ref, kn_ref, kr_ref, vt_ref, o_ref, m_ref, l_ref, a_ref, *, tk):
    qb = pl.program_id(1)
    qs = qb * Q_TILE
    nfull = qs // tk
    t_iota = lax.broadcasted_iota(I32, (1, Q_TILE), 1)
    k_iota = lax.broadcasted_iota(I32, (tk, 1), 0)
    diag0 = pl.multiple_of(nfull * tk, tk)
    diag_mask = jnp.where((k_iota + diag0) <= (t_iota + qs), 0.0, NEG_INF)
    _flash_reset(m_ref, l_ref, a_ref)

    def tile(r0, mask):
        kr = kr_ref[pl.ds(r0, tk), :]
        logits = []
        for h in range(H_B):
            kh = jnp.concatenate([kn_ref[pl.ds(r0, tk), h * D_NOPE:(h + 1) * D_NOPE], kr], axis=1)
            qh = jnp.concatenate([qn_ref[:, h * D_NOPE:(h + 1) * D_NOPE], qr_ref[:, h * LANES:(h + 1) * LANES]], axis=1)
            s = _dot_nt(kh, qh)
            logits.append(s if mask is None else s + mask)
        _flash_tile(m_ref, l_ref, a_ref, logits, lambda h: vt_ref[h * D_V:(h + 1) * D_V, pl.ds(r0, tk)])

    def body(j, c):
        tile(pl.multiple_of(j * tk, tk), None)
        return c

    lax.fori_loop(0, nfull, body, 0)
    tile(diag0, diag_mask)
    for h in range(H_B):
        o_ref[:, h * D_V:(h + 1) * D_V] = (a_ref[h] / jnp.maximum(l_ref[h], 1e-30)).T.astype(BF16)


def _mla_prompt(qnope, qrope, knope, krp, vt, n_seq, seq):
    nq = seq // Q_TILE
    tk = min(K_TILE, seq)
    per_seq = lambda a: pl.BlockSpec((None,) + a.shape[1:], lambda n, i: (n,) + (0,) * (a.ndim - 1),
                                     pipeline_mode=pl.Buffered(1))
    qspec = lambda w: pl.BlockSpec((Q_TILE, w), lambda n, i: (n * nq + i, 0))
    stat = pltpu.VMEM((H_B, 1, Q_TILE), F32)
    return pl.pallas_call(
        functools.partial(_mla_prompt_kernel, tk=tk),
        grid=(n_seq, nq),
        in_specs=[qspec(H_B * D_NOPE), qspec(H_B * LANES), per_seq(knope), per_seq(krp), per_seq(vt)],
        out_specs=qspec(H_B * D_V),
        out_shape=jax.ShapeDtypeStruct((n_seq * seq, H_B * D_V), BF16),
        scratch_shapes=[stat, stat, pltpu.VMEM((H_B, D_V, Q_TILE), F32)],
        compiler_params=_cparams(("arbitrary", "arbitrary")),
        name="mla_prompt",
    )(qnope, qrope, knope, krp, vt)


def _outproj_kernel(oa_ref, ob_ref, w_ref, x_ref, gt_ref, sc_ref, sh_ref, g_ref, x1_ref, h_ref):
    half = oa_ref.shape[1]
    mixed = _dot(oa_ref[...], w_ref[0:half, :]) + _dot(ob_ref[...], w_ref[half:2 * half, :])
    x1 = x_ref[...] + gt_ref[...] * mixed
    x1_ref[...] = x1
    h_ref[...] = (_rms(x1, g_ref[...]) * (1.0 + sc_ref[...]) + sh_ref[...]).astype(BF16)


def _outproj(oa, ob, w_out, x2d, gt, sc, sh, g_ffn, rows_per_group):
    rows, d = x2d.shape
    tm = min(512, rows)
    row = lambda n: pl.BlockSpec((tm, n), lambda i: (i, 0))
    ms = _mod_spec(gt, tm, rows_per_group)
    return pl.pallas_call(
        _outproj_kernel,
        grid=(rows // tm,),
        in_specs=[row(oa.shape[1]), row(ob.shape[1]), _const_spec(w_out.shape), row(d), ms, ms, ms,
                  _const_spec((1, d))],
        out_specs=[row(d), row(d)],
        out_shape=[jax.ShapeDtypeStruct((rows, d), F32), jax.ShapeDtypeStruct((rows, d), BF16)],
        compiler_params=_cparams(("arbitrary",)),
        name="outproj",
    )(oa, ob, w_out, x2d, gt, sc, sh, g_ffn.reshape(1, d))


def _ffn_kernel(h_ref, wg_ref, wu_ref, wd_ref, x1_ref, gt_ref, g_ref, y_ref, acc_ref):
    j = pl.program_id(1)
    h = h_ref[...]
    a = _dot(h, wg_ref[...])
    u = _dot(h, wu_ref[...])
    part = _dot(((a * jax.nn.sigmoid(a)) * u).astype(BF16), wd_ref[...])

    @pl.when(j == 0)
    def _():
        acc_ref[...] = part

    @pl.when(j > 0)
    def _():
        acc_ref[...] += part

    @pl.when(j == pl.num_programs(1) - 1)
    def _():
        x2 = x1_ref[...] + gt_ref[...] * acc_ref[...]
        y_ref[...] = _rms(x2, g_ref[...])


def _ffn(h2, wg, wu, wd, x1, gt, g_final, rows_per_group):
    rows, d = x1.shape
    dff = wg.shape[1]
    tm = min(512, rows)
    tf = 512
    row = lambda n: pl.BlockSpec((tm, n), lambda i, j: (i, 0))
    return pl.pallas_call(
        _ffn_kernel,
        grid=(rows // tm, dff // tf),
        in_specs=[row(d), pl.BlockSpec((d, tf), lambda i, j: (0, j)), pl.BlockSpec((d, tf), lambda i, j: (0, j)),
                  pl.BlockSpec((tf, d), lambda i, j: (j, 0)), row(d), _mod_spec(gt, tm, rows_per_group),
                  pl.BlockSpec((1, d), lambda i, j: (0, 0))],
        out_specs=row(d),
        out_shape=jax.ShapeDtypeStruct((rows, d), F32),
        scratch_shapes=[pltpu.VMEM((tm, d), F32)],
        compiler_params=_cparams(("arbitrary", "arbitrary")),
        name="ffn",
    )(h2, wg, wu, wd, x1, gt, g_final.reshape(1, d))


def _headproj_kernel(x_ref, w_ref, o_ref):
    o_ref[...] = _dot(x_ref[...].astype(BF16), w_ref[...])


def _headproj(x, w):
    rows = x.shape[0]
    nh, k, n = w.shape
    return pl.pallas_call(
        _headproj_kernel,
        grid=(nh,),
        in_specs=[pl.BlockSpec((rows, k), lambda h: (0, h)), pl.BlockSpec((None, k, n), lambda h: (h, 0, 0))],
        out_specs=pl.BlockSpec((rows, n), lambda h: (0, h)),
        out_shape=jax.ShapeDtypeStruct((rows, nh * n), F32),
        compiler_params=_cparams(("arbitrary",)),
        name="headproj",
    )(x, w)


def _chunk_page_copy(cache_ref, pt_ref, buf_ref, sem_ref, step, nchunk, cpages, ring, i):
    seq = step // nchunk
    c = step % nchunk
    slot = step % ring
    page = pt_ref[seq, c * cpages + i]
    return pltpu.make_async_copy(cache_ref.at[pl.ds(pl.multiple_of(page * PAGE_ROWS, PAGE_ROWS), PAGE_ROWS), :],
                                 buf_ref.at[slot, pl.ds(pl.multiple_of(i * PAGE_ROWS, PAGE_ROWS), PAGE_ROWS), :],
                                 sem_ref.at[slot])


def _ring_schedule(start_chunk, s, nstep, ring):
    ahead = ring - 1

    @pl.when(s == 0)
    def _():
        for d in range(ahead):
            @pl.when(d < nstep)
            def _():
                start_chunk(d)

    @pl.when(s + ahead < nstep)
    def _():
        start_chunk(s + ahead)


def _cmp_sample_kernel(pt_ref, cache_ref, q_ref, w1k_ref, w2k_ref, w1v_ref, w2v_ref, tbl_ref,
                       ocmp_ref, sel_ref, buf_ref, sem_ref, stage_ref, *, nchunk, cpages, k_sel, nbk, pitch, past):
    s = pl.program_id(0)
    nstep = pl.num_programs(0)
    c = s % nchunk
    slot = s % CMP_RING

    def start_chunk(step):
        def body(i, cc):
            _chunk_page_copy(cache_ref, pt_ref, buf_ref, sem_ref, step, nchunk, cpages, CMP_RING, i).start()
            return cc
        lax.fori_loop(0, cpages, body, 0)

    _ring_schedule(start_chunk, s, nstep, CMP_RING)

    def wait_body(i, cc):
        _chunk_page_copy(cache_ref, pt_ref, buf_ref, sem_ref, s, nchunk, cpages, CMP_RING, i).wait()
        return cc
    lax.fori_loop(0, cpages, wait_body, 0)

    for i in range(cpages):
        for half in range(2):
            blk = (c * cpages + i) * 2 + half
            for comp in range(2):
                slab = buf_ref[slot, pl.ds(i * PAGE_ROWS + half * BLK_ROWS + comp, BLK, stride=NSA_ROWS), :]
                stage_ref[comp, pl.ds(blk, BLK, stride=pitch), :] = slab

    @pl.when(c == nchunk - 1)
    def _():
        group = 16

        def compress(comp, w1_ref, w2_ref):
            pre = jnp.zeros((nbk, DH_A), F32)
            for g in range(BLK // group):
                x = jnp.concatenate([stage_ref[comp, pl.ds((g * group + j) * pitch, nbk), :] for j in range(group)],
                                    axis=1).astype(BF16)
                pre = pre + _dot(x, w1_ref[g * group * DH_A:(g + 1) * group * DH_A, :])
            return _dot(jax.nn.gelu(pre).astype(BF16), w2_ref[...]).astype(BF16)

        ck = compress(0, w1k_ref, w2k_ref)
        cv = compress(1, w1v_ref, w2v_ref)
        q = q_ref[...]
        rows = q.shape[0]
        nt = rows // H_A
        t_of_row = lax.broadcasted_iota(I32, (rows, 1), 0) // H_A
        b_iota = lax.broadcasted_iota(I32, (1, nbk), 1)
        n_cmp = jnp.maximum(past + t_of_row - (b_iota * BLK + (BLK - 1)), 0)
        lg = _dot_nt(q, ck) + _bias_from_cols(n_cmp, tbl_ref)
        m = jnp.max(lg, axis=-1, keepdims=True)
        p = jnp.exp2(lg - m)
        p = p / jnp.maximum(jnp.sum(p, axis=-1, keepdims=True), 1e-30)
        ocmp_ref[...] = _dot(p.astype(BF16), cv)

        width = sel_ref.shape[-1] * (-(-(nbk + 1) // sel_ref.shape[-1]))
        bid = lax.broadcasted_iota(I32, (1, width), 1).astype(F32)
        imp = jnp.concatenate([jnp.sum(p[t * H_A:(t + 1) * H_A], axis=0, keepdims=True) for t in range(nt)], axis=0)
        imp = jnp.concatenate([imp, jnp.zeros((nt, width - nbk), F32)], axis=1)
        cur = float(nbk)
        forced = (bid == 0.0) | (bid == cur) | (bid == cur - 1.0)
        score = jnp.where(forced, FORCED_SCORE, imp)
        score = jnp.where(bid > cur, NEG_INF, score)
        col = lax.broadcasted_iota(I32, (nt, LANES), 1)
        sel = jnp.zeros((nt, LANES), F32)
        for r in range(k_sel):
            mx = jnp.max(score, axis=-1, keepdims=True)
            idx = jnp.min(jnp.where(score == mx, bid, 1e9), axis=-1, keepdims=True)
            sel = jnp.where(col == r, idx, sel)
            score = jnp.where(bid == idx, NEG_INF, score)
        sel_ref[...] = sel.astype(I32)


def _cmp_sample(page_table, cache2d, q32, w1k, w2k, w1v, w2v, tbl_rows, past):
    nseq, npages = page_table.shape
    rows = q32.shape[1]
    nt = rows // H_A
    nbk = past // BLK
    k_sel = min(N_SEL, nbk + 1)
    cpages = min(CHUNK_PAGES, npages)
    nchunk = npages // cpages
    pitch = nbk + 8
    kern = functools.partial(_cmp_sample_kernel, nchunk=nchunk, cpages=cpages, k_sel=k_sel, nbk=nbk, pitch=pitch,
                             past=past)
    cst = lambda a: pl.BlockSpec(a.shape, lambda s, pt: (0,) * a.ndim, pipeline_mode=pl.Buffered(1))
    per = lambda a: pl.BlockSpec((None,) + a.shape[1:], lambda s, pt: (s // nchunk,) + (0,) * (a.ndim - 1))
    grid_spec = pltpu.PrefetchScalarGridSpec(
        num_scalar_prefetch=1,
        grid=(nseq * nchunk,),
        in_specs=[pl.BlockSpec(memory_space=pl.ANY), per(q32), cst(w1k), cst(w2k), cst(w1v), cst(w2v), cst(tbl_rows)],
        out_specs=[pl.BlockSpec((None, rows, DH_A), lambda s, pt: (s // nchunk, 0, 0)),
                   pl.BlockSpec((None, nt, LANES), lambda s, pt: (s // nchunk, 0, 0))],
        scratch_shapes=[pltpu.VMEM((CMP_RING, cpages * PAGE_ROWS, DH_A), F32), pltpu.SemaphoreType.DMA((CMP_RING,)),
                        pltpu.VMEM((2, BLK * pitch, DH_A), F32)],
    )
    return pl.pallas_call(
        kern,
        grid_spec=grid_spec,
        out_shape=[jax.ShapeDtypeStruct((nseq, rows, DH_A), F32), jax.ShapeDtypeStruct((nseq, nt, LANES), I32)],
        compiler_params=_cparams(("arbitrary",)),
        name="cmp_sample",
    )(page_table, cache2d, q32, w1k, w2k, w1v, w2v, tbl_rows)


def _slc_copy(cache_ref, pt_ref, sel_ref, buf_ref, sem_ref, seq, slot, i, nbk):
    blk = jnp.minimum(sel_ref[seq, i], nbk - 1)
    r0 = pl.multiple_of(pt_ref[seq, blk // 2] * PAGE_ROWS + (blk % 2) * BLK_ROWS, BLK_ROWS)
    return pltpu.make_async_copy(cache_ref.at[pl.ds(r0, BLK_ROWS), :], buf_ref.at[slot, i], sem_ref.at[slot])


def _slc_sample_kernel(pt_ref, sel_ref, cache_ref, q_ref, rel_ref, new_ref, state_ref, wnew_ref, tbl_ref,
                       ocmp_ref, g0_ref, g1_ref, g2_ref, o_ref, buf_ref, sem_ref, *, nbk, nslot, nt, wb):
    b = pl.program_id(0)
    nseq = pl.num_programs(0)
    slot = b % 2
    ntot = nt * nslot

    def start_all(seq, sl):
        def body(i, c):
            @pl.when(sel_ref[seq, i] < nbk)
            def _():
                _slc_copy(cache_ref, pt_ref, sel_ref, buf_ref, sem_ref, seq, sl, i, nbk).start()
            return c
        lax.fori_loop(0, ntot, body, 0)

    @pl.when(b == 0)
    def _():
        start_all(0, 0)

    @pl.when(b + 1 < nseq)
    def _():
        start_all(b + 1, 1 - slot)

    def wait_body(i, c):
        @pl.when(sel_ref[b, i] < nbk)
        def _():
            _slc_copy(cache_ref, pt_ref, sel_ref, buf_ref, sem_ref, b, slot, i, nbk).wait()

        @pl.when(sel_ref[b, i] >= nbk)
        def _():
            buf_ref[slot, i] = new_ref[...]
        return c
    lax.fori_loop(0, ntot, wait_body, 0)

    kst = state_ref[pl.ds(0, wb, stride=2), :].astype(BF16)
    vst = state_ref[pl.ds(1, wb, stride=2), :].astype(BF16)
    nwn = wnew_ref.shape[0] // 2
    kwn = wnew_ref[pl.ds(0, nwn, stride=2), :].astype(BF16)
    vwn = wnew_ref[pl.ds(1, nwn, stride=2), :].astype(BF16)
    st_iota = lax.broadcasted_iota(I32, (1, wb), 1)
    nw_iota = lax.broadcasted_iota(I32, (1, nwn), 1)
    for t in range(nt):
        q = q_ref[t * H_A:(t + 1) * H_A, :]
        blocks = buf_ref.at[slot]
        ks = jnp.concatenate([blocks[t * nslot + r, pl.ds(2, BLK, stride=NSA_ROWS), :] for r in range(nslot)], axis=0)
        vs = jnp.concatenate([blocks[t * nslot + r, pl.ds(3, BLK, stride=NSA_ROWS), :] for r in range(nslot)], axis=0)
        rel = rel_ref[t:t + 1, :]
        lg = _dot_nt(q, ks.astype(BF16)) + _bias_from_cols(jnp.maximum(rel, 0), tbl_ref)
        ok = rel >= 0
        lg = jnp.where(ok, lg, NEG_INF)
        m = jnp.max(lg, axis=-1, keepdims=True)
        p = jnp.exp2(lg - m) * ok.astype(F32)
        p = p / jnp.maximum(jnp.sum(p, axis=-1, keepdims=True), 1e-30)
        o_slc = _dot(p.astype(BF16), vs.astype(BF16))

        rel_s = wb + t - st_iota
        rel_n = t - nw_iota
        ls = _dot_nt(q, kst) + _bias_from_cols(rel_s, tbl_ref)
        ln = _dot_nt(q, kwn) + _bias_from_cols(jnp.maximum(rel_n, 0), tbl_ref)
        ok_s = rel_s <= WINDOW
        ok_n = rel_n >= 0
        ls = jnp.where(ok_s, ls, NEG_INF)
        ln = jnp.where(ok_n, ln, NEG_INF)
        m = jnp.maximum(jnp.max(ls, axis=-1, keepdims=True), jnp.max(ln, axis=-1, keepdims=True))
        ps = jnp.exp2(ls - m) * ok_s.astype(F32)
        pn = jnp.exp2(ln - m) * ok_n.astype(F32)
        den = jnp.maximum(jnp.sum(ps, axis=-1, keepdims=True) + jnp.sum(pn, axis=-1, keepdims=True), 1e-30)
        o_win = _dot((ps / den).astype(BF16), vst) + _dot((pn / den).astype(BF16), vwn)
        r = slice(t * H_A, (t + 1) * H_A)
        o_ref[r, :] = g0_ref[r, :] * ocmp_ref[r, :] + g1_ref[r, :] * o_slc + g2_ref[r, :] * o_win


def _slc_sample(page_table, sel_flat, cache2d, q32, rel, new_blk, state, wnew, tbl_t, ocmp, g0, g1, g2, nbk, nslot):
    nseq = page_table.shape[0]
    rows = q32.shape[1]
    nt = rows // H_A
    wb = state.shape[1] // 2
    per = lambda a: pl.BlockSpec((None,) + a.shape[1:], lambda b, *_: (b,) + (0,) * (a.ndim - 1))
    kern = functools.partial(_slc_sample_kernel, nbk=nbk, nslot=nslot, nt=nt, wb=wb)
    grid_spec = pltpu.PrefetchScalarGridSpec(
        num_scalar_prefetch=2,
        grid=(nseq,),
        in_specs=[pl.BlockSpec(memory_space=pl.ANY), per(q32), per(rel), per(new_blk), per(state), per(wnew),
                  pl.BlockSpec(tbl_t.shape, lambda b, *_: (0, 0)), per(ocmp), per(g0), per(g1), per(g2)],
        out_specs=pl.BlockSpec((None, rows, DH_A), lambda b, *_: (b, 0, 0)),
        scratch_shapes=[pltpu.VMEM((2, nt * nslot, BLK_ROWS, DH_A), F32), pltpu.SemaphoreType.DMA((2,))],
    )
    return pl.pallas_call(
        kern,
        grid_spec=grid_spec,
        out_shape=jax.ShapeDtypeStruct((nseq, rows, DH_A), F32),
        compiler_params=_cparams(("arbitrary",)),
        name="slc_sample",
    )(page_table, sel_flat, cache2d, q32, rel, new_blk, state, wnew, tbl_t, ocmp, g0, g1, g2)


def _mla_page_copy(cache_ref, pt_ref, buf_ref, sem_ref, step, nchunk, cpages, i):
    seq = step // nchunk
    c = step % nchunk
    slot = step % MLA_RING
    page = pt_ref[seq, c * cpages + i]
    return pltpu.make_async_copy(cache_ref.at[page],
                                 buf_ref.at[slot, :, pl.ds(pl.multiple_of(i * PAGE_SIZE, PAGE_SIZE), PAGE_SIZE)],
                                 sem_ref.at[slot])


def _mla_sample_kernel(pt_ref, cache_ref, q_ref, new_ref, o_ref, buf_ref, sem_ref, m_ref, l_ref, acc_ref,
                       *, nchunk, cpages):
    s = pl.program_id(0)
    nstep = pl.num_programs(0)
    c = s % nchunk
    slot = s % MLA_RING

    def start_chunk(step):
        def body(i, cc):
            _mla_page_copy(cache_ref, pt_ref, buf_ref, sem_ref, step, nchunk, cpages, i).start()
            return cc
        lax.fori_loop(0, cpages, body, 0)

    _ring_schedule(start_chunk, s, nstep, MLA_RING)
    q = q_ref[...]

    @pl.when(c == 0)
    def _():
        new = new_ref[...]
        sn = _dot_nt(q, new.astype(BF16))
        rows = sn.shape[0]
        t_of_row = lax.broadcasted_iota(I32, (rows, 1), 0) // H_B
        j = lax.broadcasted_iota(I32, (1, new.shape[0]), 1)
        ok = j <= t_of_row
        sn = jnp.where(ok, sn, NEG_INF)
        m = jnp.max(sn, axis=-1, keepdims=True)
        p = jnp.exp2(sn - m) * ok.astype(F32)
        m_ref[...] = m
        l_ref[...] = jnp.sum(p, axis=-1, keepdims=True)
        acc_ref[...] = _dot(p.astype(BF16), new[:, 0:KV_LORA].astype(BF16))

    def wait_body(i, cc):
        _mla_page_copy(cache_ref, pt_ref, buf_ref, sem_ref, s, nchunk, cpages, i).wait()
        return cc
    lax.fori_loop(0, cpages, wait_body, 0)

    kv = buf_ref[slot].astype(BF16)
    sc = _dot(q, kv)
    m_old = m_ref[...]
    m_new = jnp.maximum(m_old, jnp.max(sc, axis=-1, keepdims=True))
    alpha = jnp.exp2(m_old - m_new)
    p = jnp.exp2(sc - m_new)
    l_new = alpha * l_ref[...] + jnp.sum(p, axis=-1, keepdims=True)
    acc = alpha * acc_ref[...] + _dot_nt(p.astype(BF16), kv[0:KV_LORA, :])
    m_ref[...] = m_new
    l_ref[...] = l_new
    acc_ref[...] = acc

    @pl.when(c == nchunk - 1)
    def _():
        o_ref[...] = acc / jnp.maximum(l_new, 1e-30)


def _mla_sample(page_table, cache_t, q576, new_rows):
    nseq, npages = page_table.shape
    rows = q576.shape[1]
    cpages = min(CHUNK_PAGES, npages)
    nchunk = npages // cpages
    kern = functools.partial(_mla_sample_kernel, nchunk=nchunk, cpages=cpages)
    per = lambda a: pl.BlockSpec((None,) + a.shape[1:], lambda s, pt: (s // nchunk,) + (0,) * (a.ndim - 1))
    grid_spec = pltpu.PrefetchScalarGridSpec(
        num_scalar_prefetch=1,
        grid=(nseq * nchunk,),
        in_specs=[pl.BlockSpec(memory_space=pl.ANY), per(q576), per(new_rows)],
        out_specs=pl.BlockSpec((None, rows, KV_LORA), lambda s, pt: (s // nchunk, 0, 0)),
        scratch_shapes=[pltpu.VMEM((MLA_RING, MLA_ROW, cpages * PAGE_SIZE), F32), pltpu.SemaphoreType.DMA((MLA_RING,)),
                        pltpu.VMEM((rows, 1), F32), pltpu.VMEM((rows, 1), F32), pltpu.VMEM((rows, KV_LORA), F32)],
    )
    return pl.pallas_call(
        kern,
        grid_spec=grid_spec,
        out_shape=jax.ShapeDtypeStruct((nseq, rows, KV_LORA), F32),
        compiler_params=_cparams(("arbitrary",)),
        name="mla_sample",
    )(page_table, cache_t, q576, new_rows)


def _prep_w_in(w_in):
    d = w_in.shape[0]
    sizes = (H_A * DH_A, 4 * DH_A, 2 * DH_A, 3 * H_A, Q_LORA, KV_LORA, D_ROPE)
    offs = np.concatenate([[0], np.cumsum(sizes)])
    seg = lambda i: w_in[:, int(offs[i]):int(offs[i + 1])]
    zq, zkv, zwin, zg, zcq, zckv, zkr = (seg(i) for i in range(7))
    half = D_ROPE // 2
    zrot = jnp.concatenate([-zkr[:, half:], zkr[:, :half]], axis=1)
    pad = lambda a: jnp.pad(a, ((0, 0), (0, LANES - a.shape[1])))
    w_ext = jnp.concatenate([zq, zkv, zwin, zcq, zckv, pad(zkr), pad(zrot), pad(zg)], axis=1)
    assert w_ext.shape == (d, C_END)
    return w_ext.astype(BF16)


def _prep_w_uq(w_uq):
    r = w_uq.shape[0]
    w = w_uq.reshape(r, H_B, D_NOPE + D_ROPE)
    nope = w[:, :, :D_NOPE].reshape(r, H_B * D_NOPE)
    rope = w[:, :, D_NOPE:]
    half = D_ROPE // 2
    rot = jnp.concatenate([-rope[:, :, half:], rope[:, :, :half]], axis=2)
    padh = lambda a: jnp.pad(a, ((0, 0), (0, 0), (0, LANES - D_ROPE))).reshape(r, H_B * LANES)
    return jnp.concatenate([nope, padh(rope), padh(rot)], axis=1).astype(BF16)


def _rope_tables(pos):
    half = D_ROPE // 2
    inv = ROPE_THETA ** (-jnp.arange(half, dtype=F32) / half)
    ang = pos.astype(F32)[:, None] * inv[None, :]
    cos = jnp.cos(ang)
    sin = jnp.sin(ang)
    z = jnp.zeros((pos.shape[0], LANES - D_ROPE), F32)
    return jnp.concatenate([cos, cos, z], axis=1), jnp.concatenate([sin, sin, z], axis=1)


def kernel(x_prompt, x_sample, cache_nsa_kv, cache_mla, state_win_kv, page_table, c_prompt, c_sample, rel_table,
           g_attn, g_ffn, g_final, w_ada, b_ada, w_in, g_qnorm, w_uq, g_kvnorm, w_uk, w_uv, w_ck1, w_ck2, w_cv1,
           w_cv2, w_out, w_gate, w_up, w_down):
    n_p, seq, d = x_prompt.shape
    n_s, t_s, _ = x_sample.shape
    npages = page_table.shape[1]
    past = npages * PAGE_SIZE
    wb = state_win_kv.shape[1]
    n_pool = cache_nsa_kv.shape[0]
    assert seq % K_TILE == 0 and seq >= WINDOW and wb == WINDOW and t_s <= 8
    assert cache_nsa_kv.shape[1:] == (PAGE_SIZE, NSA_ROWS, DH_A) and cache_mla.shape[1:] == (PAGE_SIZE, MLA_ROW)

    w_ext = _prep_w_in(w_in)
    wuq_ext = _prep_w_uq(w_uq)
    wkv = jnp.concatenate([w_uk.reshape(KV_LORA, H_B * D_NOPE), w_uv.reshape(KV_LORA, H_B * D_V)], axis=1).astype(BF16)
    w_out_b = w_out.astype(BF16)
    wg_b, wu_b, wd_b = w_gate.astype(BF16), w_up.astype(BF16), w_down.astype(BF16)
    w1k_b, w2k_b, w1v_b, w2v_b = (w.astype(BF16) for w in (w_ck1, w_ck2, w_cv1, w_cv2))

    n_c = n_p + n_s
    c_all = jnp.pad(jnp.concatenate([c_prompt, c_sample], axis=0), ((0, (-n_c) % 8), (0, 0)))
    mod = _modulation(c_all, w_ada, b_ada)
    mod_p = mod[:n_p].reshape(n_p, 6, 1, d)
    mod_s = jnp.repeat(mod[n_p:n_c].reshape(n_s, 6, d), t_s, axis=0).transpose(1, 0, 2)[:, None]
    sh_a, sc_a, gt_a, sh_f, sc_f, gt_f = (mod_p[:, i] for i in range(6))
    sh_a_s, sc_a_s, gt_a_s, sh_f_s, sc_f_s, gt_f_s = (mod_s[i] for i in range(6))

    cos_p, sin_p = _rope_tables(jnp.arange(seq, dtype=I32))
    pos_s = past + jnp.tile(jnp.arange(t_s, dtype=I32), n_s)
    cos_s, sin_s = _rope_tables(pos_s)
    xp2 = x_prompt.reshape(n_p * seq, d)
    xs2 = x_sample.reshape(n_s * t_s, d)
    (qn_p, nsa_p, win_p, gates_p, mla_p, qnope_p, qrope_p, krp_p, knope_p, v_p) = _inproj(
        xp2, sc_a, sh_a, g_attn, w_ext, g_qnorm, wuq_ext, g_kvnorm, wkv, cos_p, sin_p, seq)
    (qn_s, nsa_s, win_s, gates_s, mla_s, qnope_s, qrope_s, _krp_s, _kn_s, _v_s) = _inproj(
        xs2, sc_a_s, sh_a_s, g_attn, w_ext, g_qnorm, wuq_ext, g_kvnorm, wkv, cos_s, sin_s, t_s)

    nb = seq // BLK
    nsa_p4 = nsa_p.reshape(n_p, seq, 4, DH_A)
    kb = nsa_p4[:, :, 0].reshape(n_p * nb, BLK * DH_A)
    vb = nsa_p4[:, :, 1].reshape(n_p * nb, BLK * DH_A)
    ck_p, cv_p = _compress_prompt(kb, vb, w1k_b, w2k_b, w1v_b, w2v_b)
    ck_p = ck_p.reshape(n_p, nb, DH_A).astype(BF16)
    cvt_p = cv_p.reshape(n_p, nb, DH_A).transpose(0, 2, 1).astype(BF16)
    gates_t = gates_p[:, :3 * H_A].reshape(n_p, seq, 3 * H_A).transpose(0, 2, 1)

    ks = nsa_p4[:, :, 2].astype(BF16)
    vs = nsa_p4[:, :, 3].astype(BF16)
    win_p4 = win_p.reshape(n_p, seq, 2, DH_A)
    kw = win_p4[:, :, 0].astype(BF16)
    vw = win_p4[:, :, 1].astype(BF16)
    ks_p = jnp.pad(ks, ((0, 0), (Q_TILE, 0), (0, 0)))
    vst_p = jnp.pad(vs, ((0, 0), (Q_TILE, 0), (0, 0))).transpose(0, 2, 1)
    kw_p = jnp.pad(kw, ((0, 0), (WINDOW, 0), (0, 0)))
    vwt_p = jnp.pad(vw, ((0, 0), (WINDOW, 0), (0, 0))).transpose(0, 2, 1)
    o_nsa_p = _nsa_prompt(rel_table, qn_p, gates_t, ck_p, cvt_p, ks_p, vst_p, kw_p, vwt_p, n_p, seq)

    o_mla_p = _mla_prompt(qnope_p, qrope_p, knope_p.reshape(n_p, seq, H_B * D_NOPE), krp_p.reshape(n_p, seq, LANES),
                          v_p.reshape(n_p, seq, H_B * D_V).transpose(0, 2, 1), n_p, seq)

    rows_s = t_s * H_A
    nbk = past // BLK
    nbs = -(-(past + t_s) // BLK)
    assert nbs == nbk + 1
    k_sel = min(N_SEL, nbs)
    cache2d = cache_nsa_kv.reshape(n_pool * PAGE_ROWS, DH_A)
    cache_t = cache_mla.transpose(0, 2, 1)
    q32 = qn_s.reshape(n_s, rows_s, DH_A)
    tbl_t = rel_table.T
    tbl_rows = jnp.tile(tbl_t, (t_s, 1))

    ocmp_s, sel_pad = _cmp_sample(page_table, cache2d, q32, w1k_b, w2k_b, w1v_b, w2v_b, tbl_rows, past)
    sel = sel_pad[:, :, :k_sel]

    qpos = past + jnp.arange(t_s, dtype=I32)
    tok = sel[..., None] * BLK + jnp.arange(BLK, dtype=I32)
    rel_slc = (qpos[None, :, None, None] - tok).reshape(n_s, t_s, k_sel * BLK)
    new_blk = jnp.pad(nsa_s.reshape(n_s, t_s * NSA_ROWS, DH_A), ((0, 0), (0, BLK_ROWS - t_s * NSA_ROWS), (0, 0)))
    wnew = jnp.pad(win_s.reshape(n_s, t_s * 2, DH_A), ((0, 0), (0, 16 - t_s * 2), (0, 0)))
    gs = gates_s[:, :3 * H_A].reshape(n_s, rows_s, 3)
    g0, g1, g2 = (jnp.broadcast_to(gs[:, :, i:i + 1], (n_s, rows_s, DH_A)) for i in range(3))
    o_nsa_s = _slc_sample(page_table, sel.reshape(n_s, t_s * k_sel), cache2d, q32, rel_slc, new_blk,
                          state_win_kv.reshape(n_s, wb * 2, DH_A), wnew, tbl_t, ocmp_s, g0, g1, g2, nbk, k_sel)

    wuk_t = w_uk.transpose(1, 2, 0).astype(BF16)
    qlat = _headproj(qnope_s, wuk_t).reshape(n_s, rows_s, KV_LORA).astype(BF16)
    qrope_s3 = qrope_s.reshape(n_s, t_s, H_B, LANES)[..., :D_ROPE].reshape(n_s, rows_s, D_ROPE)
    q576 = jnp.concatenate([qlat, qrope_s3], axis=-1)
    mla_new = jnp.pad(mla_s.reshape(n_s, t_s, MLA_ROW), ((0, 0), (0, 8 - t_s), (0, 0)))
    o_lat = _mla_sample(page_table, cache_t, q576, mla_new)
    wuv_h = w_uv.transpose(1, 0, 2).astype(BF16)
    o_mla_s = _headproj(o_lat.reshape(n_s * t_s, H_B * KV_LORA), wuv_h)

    x1_p, h2_p = _outproj(o_nsa_p, o_mla_p, w_out_b, xp2, gt_a, sc_f, sh_f, g_ffn, seq)
    y_p = _ffn(h2_p, wg_b, wu_b, wd_b, x1_p, gt_f, g_final, seq)
    oa_s = o_nsa_s.reshape(n_s * t_s, H_A * DH_A).astype(BF16)
    x1_s, h2_s = _outproj(oa_s, o_mla_s.astype(BF16), w_out_b, xs2, gt_a_s, sc_f_s, sh_f_s, g_ffn, t_s)
    y_s = _ffn(h2_s, wg_b, wu_b, wd_b, x1_s, gt_f_s, g_final, t_s)

    win_all = jnp.concatenate([state_win_kv, win_s.reshape(n_s, t_s, 2, DH_A)], axis=1)
    return (y_p.reshape(n_p, seq, d), y_s.reshape(n_s, t_s, d),
            nsa_p4, mla_p.reshape(n_p, seq, MLA_ROW), win_p4[:, -min(WINDOW, seq):],
            nsa_s.reshape(n_s, t_s, 4, DH_A), mla_s.reshape(n_s, t_s, MLA_ROW),
            win_all[:, -min(WINDOW, wb + t_s):])
```

```python
import functools
import math

import numpy as np
import jax
import jax.numpy as jnp
from jax import lax
from jax.experimental import pallas as pl
from jax.experimental.pallas import tpu as pltpu

F32 = jnp.float32
BF16 = jnp.bfloat16
I32 = jnp.int32

H_A = 8
DH_A = 128
BLK = 64
N_SEL = 16
WINDOW = 512
H_B = 8
D_NOPE = 128
D_ROPE = 64
D_V = 128
Q_LORA = 512
KV_LORA = 512
MLA_ROW = KV_LORA + D_ROPE
PAGE_SIZE = 128
ROPE_THETA = 10000.0
N_BUCKETS = 32
T5_MAX_DIST = 128
LOG2E = math.log2(math.e)
QSCALE_A = DH_A ** -0.5 * LOG2E
QSCALE_B = (D_NOPE + D_ROPE) ** -0.5 * LOG2E
EPS = 1e-6
NEG_INF = -1e30
FORCED_SCORE = 1e4

Q_TILE = 128
K_TILE = 512
LANES = 128
V7X_VMEM_LIMIT = 56 * 1024 * 1024
NSA_ROWS = 4
PAGE_ROWS = PAGE_SIZE * NSA_ROWS
BLK_ROWS = BLK * NSA_ROWS
CHUNK_PAGES = 32
CMP_RING = 3
MLA_RING = 3
DMA_UNROLL = 4


def _t5_bucket_np(n):
    n = np.maximum(np.asarray(n, np.int64), 0)
    exact = N_BUCKETS // 2
    nf = np.maximum(n, 1).astype(np.float32)
    far = exact + (np.log(nf / np.float32(exact)) / np.float32(math.log(T5_MAX_DIST / exact))
                   * np.float32(N_BUCKETS - exact)).astype(np.int32)
    return np.where(n < exact, n, np.minimum(far, N_BUCKETS - 1)).astype(np.int32)


_BUCKET_START = [int(np.argmax(_t5_bucket_np(np.arange(4 * T5_MAX_DIST)) >= b)) for b in range(N_BUCKETS)]


def _dot(a, b):
    return jnp.dot(a, b, preferred_element_type=F32)


def _dot_nt(a, b):
    return lax.dot_general(a, b, (((1,), (1,)), ((), ())), preferred_element_type=F32)


def _rms(x, g):
    return x * lax.rsqrt(jnp.mean(x * x, axis=-1, keepdims=True) + EPS) * g


def _cparams(sem, vmem=V7X_VMEM_LIMIT):
    return pltpu.CompilerParams(dimension_semantics=sem, vmem_limit_bytes=vmem)


def _const_spec(shape):
    nd = len(shape)
    return pl.BlockSpec(shape, lambda *_: (0,) * nd, pipeline_mode=pl.Buffered(1))


def _smem_spec():
    return pl.BlockSpec(memory_space=pltpu.SMEM)


def _bias_from_smem(n, tbl_ref, h, offset=0.0):
    bias = jnp.full(n.shape, 0.0, F32) + (tbl_ref[N_BUCKETS - 1, h] - offset) * LOG2E
    for b in range(N_BUCKETS - 2, -1, -1):
        bias = jnp.where(n < _BUCKET_START[b + 1], (tbl_ref[b, h] - offset) * LOG2E, bias)
    return bias


def _bias_from_cols(n, tbl_ref):
    bias = jnp.broadcast_to(tbl_ref[:, N_BUCKETS - 1:N_BUCKETS], (tbl_ref.shape[0], n.shape[1]))
    for b in range(N_BUCKETS - 2, -1, -1):
        bias = jnp.where(n < _BUCKET_START[b + 1], tbl_ref[:, b:b + 1], bias)
    return bias * LOG2E


def _mod_kernel(c_ref, w_ref, b_ref, o_ref):
    c = c_ref[...]
    s = (c * jax.nn.sigmoid(c)).astype(BF16)
    o_ref[...] = _dot(s, w_ref[...].astype(BF16)) + b_ref[...]


def _modulation(c_all, w_ada, b_ada):
    m, d = c_all.shape
    n = w_ada.shape[1]
    tn = 512
    return pl.pallas_call(
        _mod_kernel,
        grid=(n // tn,),
        in_specs=[pl.BlockSpec((m, d), lambda j: (0, 0)),
                  pl.BlockSpec((d, tn), lambda j: (0, j)),
                  pl.BlockSpec((1, tn), lambda j: (0, j))],
        out_specs=pl.BlockSpec((m, tn), lambda j: (0, j)),
        out_shape=jax.ShapeDtypeStruct((m, n), F32),
        compiler_params=_cparams(("arbitrary",)),
        name="adaln_mod",
    )(c_all, w_ada, b_ada.reshape(1, n))


C_QN = 0
C_KV = C_QN + H_A * DH_A
C_WIN = C_KV + 4 * DH_A
C_CQ = C_WIN + 2 * DH_A
C_CKV = C_CQ + Q_LORA
C_KR = C_CKV + KV_LORA
C_KRR = C_KR + LANES
C_G = C_KRR + LANES
C_END = C_G + LANES


def _inproj_kernel(x_ref, sc_ref, sh_ref, g_ref, w_ref, gq_ref, wuq_ref, gkv_ref, wkv_ref, cos_ref, sin_ref,
                   qn_ref, nsa_ref, win_ref, gates_ref, mla_ref, qnope_ref, qrope_ref, krp_ref, knope_ref, v_ref,
                   kvb_ref):
    x = x_ref[...]
    h = _rms(x, g_ref[...]) * (1.0 + sc_ref[...]) + sh_ref[...]
    z = _dot(h.astype(BF16), w_ref[...])
    qn_ref[...] = (z[:, C_QN:C_KV] * QSCALE_A).astype(BF16)
    nsa_ref[...] = z[:, C_KV:C_WIN]
    win_ref[...] = z[:, C_WIN:C_CQ]
    kvb_ref[...] = z[:, C_KV:C_CQ].astype(BF16)
    gates_ref[...] = jax.nn.sigmoid(z[:, C_G:C_END])
    cos = cos_ref[...]
    sin = sin_ref[...]
    kvn = _rms(z[:, C_CKV:C_KR], gkv_ref[...])
    krp = z[:, C_KR:C_KRR] * cos + z[:, C_KRR:C_G] * sin
    mla_ref[:, 0:KV_LORA] = kvn
    mla_ref[:, KV_LORA:MLA_ROW] = krp[:, 0:D_ROPE]
    krp_ref[...] = krp.astype(BF16)
    cqn = _rms(z[:, C_CQ:C_CKV], gq_ref[...])
    q = _dot(cqn.astype(BF16), wuq_ref[...])
    nq = H_B * D_NOPE
    qnope_ref[...] = (q[:, 0:nq] * QSCALE_B).astype(BF16)
    for hh in range(H_B):
        a = q[:, nq + hh * LANES: nq + (hh + 1) * LANES]
        b = q[:, 2 * nq + hh * LANES: 2 * nq + (hh + 1) * LANES]
        qrope_ref[:, hh * LANES:(hh + 1) * LANES] = ((a * cos + b * sin) * QSCALE_B).astype(BF16)
    kv = _dot(kvn.astype(BF16), wkv_ref[...])
    knope_ref[...] = kv[:, 0:nq].astype(BF16)
    v_ref[...] = kv[:, nq:2 * nq].astype(BF16)


def _mod_spec(arr, tm, rows_per_group):
    d = arr.shape[-1]
    if arr.shape[1] != 1:
        return pl.BlockSpec((None, tm, d), lambda i, *_: (0, i, 0))
    tps = rows_per_group // tm
    return pl.BlockSpec((None, 1, d), lambda i, *_: (i // tps, 0, 0))


def _inproj(x2d, sc, sh, g_attn, w_ext, g_q, wuq_ext, g_kv, wkv, cos_t, sin_t, rows_per_group):
    rows, d = x2d.shape
    tm = min(512, rows)
    nt = rows // tm
    mod_spec = _mod_spec(sc, tm, rows_per_group)
    ptiles = cos_t.shape[0] // tm
    pos_spec = pl.BlockSpec((tm, LANES), lambda i: (i % ptiles, 0))
    row = lambda n: pl.BlockSpec((tm, n), lambda i: (i, 0))
    nq = H_B * D_NOPE
    outs = [(H_A * DH_A, BF16), (4 * DH_A, F32), (2 * DH_A, F32), (LANES, F32), (MLA_ROW, F32),
            (nq, BF16), (H_B * LANES, BF16), (LANES, BF16), (nq, BF16), (nq, BF16), (6 * DH_A, BF16)]
    return pl.pallas_call(
        _inproj_kernel,
        grid=(nt,),
        in_specs=[row(d), mod_spec, mod_spec, _const_spec((1, d)), _const_spec(w_ext.shape),
                  _const_spec((1, Q_LORA)), _const_spec(wuq_ext.shape), _const_spec((1, KV_LORA)),
                  _const_spec(wkv.shape), pos_spec, pos_spec],
        out_specs=[row(n) for n, _ in outs],
        out_shape=[jax.ShapeDtypeStruct((rows, n), dt) for n, dt in outs],
        compiler_params=_cparams(("arbitrary",)),
        name="inproj",
    )(x2d, sc, sh, g_attn.reshape(1, d), w_ext, g_q.reshape(1, Q_LORA), wuq_ext, g_kv.reshape(1, KV_LORA),
      wkv, cos_t, sin_t)


def _compress_kernel(kb_ref, vb_ref, w1k_ref, w2k_ref, w1v_ref, w2v_ref, ck_ref, cv_ref):
    hk = jax.nn.gelu(_dot(kb_ref[...].astype(BF16), w1k_ref[...]))
    ck_ref[...] = _dot(hk.astype(BF16), w2k_ref[...])
    hv = jax.nn.gelu(_dot(vb_ref[...].astype(BF16), w1v_ref[...]))
    cv_ref[...] = _dot(hv.astype(BF16), w2v_ref[...])


def _compress_prompt(kb, vb, w1k, w2k, w1v, w2v):
    nblk = kb.shape[0]
    full = lambda a: pl.BlockSpec(a.shape, lambda i: (0,) * a.ndim)
    args = (kb, vb, w1k, w2k, w1v, w2v)
    return pl.pallas_call(
        _compress_kernel,
        grid=(1,),
        in_specs=[full(a) for a in args],
        out_specs=[pl.BlockSpec((nblk, DH_A), lambda i: (0, 0))] * 2,
        out_shape=[jax.ShapeDtypeStruct((nblk, DH_A), F32)] * 2,
        compiler_params=_cparams(("arbitrary",)),
        name="compress_prompt",
    )(*args)


def _flash_reset(m_ref, l_ref, a_ref):
    m_ref[...] = jnp.full(m_ref.shape, NEG_INF, F32)
    l_ref[...] = jnp.zeros(l_ref.shape, F32)
    a_ref[...] = jnp.zeros(a_ref.shape, F32)


def _flash_tile(m_ref, l_ref, a_ref, logits, v_of):
    probs = []
    for h, s in enumerate(logits):
        m = m_ref[h]
        m_new = jnp.maximum(m, jnp.max(s, axis=0, keepdims=True))
        alpha = jnp.exp2(m - m_new)
        p = jnp.exp2(s - m_new)
        l_ref[h] = alpha * l_ref[h] + jnp.sum(p, axis=0, keepdims=True)
        m_ref[h] = m_new
        probs.append((alpha, p.astype(BF16)))
    for h, (alpha, p) in enumerate(probs):
        a_ref[h] = alpha * a_ref[h] + _dot(v_of(h), p)


def _nsa_prompt_kernel(tbl_ref, q_ref, gt_ref, ck_ref, cvt_ref, ks_ref, vst_ref, kw_ref, vwt_ref, o_ref,
                       out_ref, m_ref, l_ref, a_ref, band_ref, cband_ref, score_ref, self_ref, selr_ref, madd_ref,
                       nearm_ref, *, nb, k_sel):
    qb = pl.program_id(1)
    qs = qb * Q_TILE
    t_iota = lax.broadcasted_iota(I32, (1, Q_TILE), 1)
    tpos = qs + t_iota
    b_iota = lax.broadcasted_iota(I32, (nb, 1), 0)
    kp_iota = lax.broadcasted_iota(I32, (2 * Q_TILE, 1), 0)

    @pl.when((pl.program_id(0) == 0) & (qb == 0))
    def _():
        n_band = jnp.maximum(t_iota - kp_iota + Q_TILE, 0)
        j_iota = lax.broadcasted_iota(I32, (cband_ref.shape[1], 1), 0)
        n_cb = jnp.maximum(t_iota - (BLK - 1) - BLK * (j_iota - 2), 0)
        for h in range(H_A):
            band_ref[h] = _bias_from_smem(n_band, tbl_ref, h, offset=tbl_ref[N_BUCKETS - 1, h])
            cband_ref[h] = _bias_from_smem(n_cb, tbl_ref, h)

    blk_end = b_iota * BLK + (BLK - 1)
    valid_c = blk_end <= tpos
    valid_cf = valid_c.astype(F32)
    j_rel = b_iota - 2 * qb + 2
    ck = ck_ref[...]
    cvt = cvt_ref[...]

    def cmp_bias(h):
        bias = jnp.full((nb, Q_TILE), 0.0, F32) + tbl_ref[N_BUCKETS - 1, h] * LOG2E
        for j in range(4):
            bias = jnp.where(j_rel == j, cband_ref[h, j:j + 1, :], bias)
        return bias

    imp = jnp.zeros((nb, Q_TILE), F32)
    lgs = [_dot_nt(ck, q_ref[:, h * DH_A:(h + 1) * DH_A]) for h in range(H_A)]
    ps = []
    for h in range(H_A):
        lg = jnp.where(valid_c, lgs[h] + cmp_bias(h), NEG_INF)
        m = jnp.max(lg, axis=0, keepdims=True)
        p = jnp.exp2(lg - m) * valid_cf
        p = p / jnp.maximum(jnp.sum(p, axis=0, keepdims=True), 1e-30)
        imp = imp + p
        ps.append(p.astype(BF16))
    for h in range(H_A):
        out_ref[h] = gt_ref[3 * h:3 * h + 1, :] * _dot(cvt, ps[h])

    cur = tpos // BLK
    forced = (b_iota == 0) | (b_iota == cur) | (b_iota == cur - 1)
    score = jnp.where(forced, FORCED_SCORE, jnp.where(b_iota <= cur, imp, -FORCED_SCORE))
    score_ref[...] = score

    def rank_body(bp, rank):
        row = score_ref[pl.ds(bp, 1), :]
        tie = jnp.where(bp < b_iota, 1, 0)
        return rank + jnp.where(row > score, 1, jnp.where(row == score, tie, 0))

    rank = lax.fori_loop(0, nb, rank_body, jnp.zeros((nb, Q_TILE), I32))
    sel = rank < k_sel
    selr_ref[...] = sel.astype(F32)
    self_ref[...] = jnp.where(sel & (b_iota < 2 * qb - 2), 0.0, NEG_INF)
    nfar = (jnp.maximum(qb - 1, 0) * Q_TILE + (K_TILE - 1)) // K_TILE
    blocks_per_tile = K_TILE // BLK

    def mask_body(j, c):
        for i in range(blocks_per_tile):
            row = self_ref[pl.ds(j * blocks_per_tile + i, 1), :]
            madd_ref[pl.ds(pl.multiple_of(j * K_TILE + i * BLK, BLK), BLK), :] = jnp.broadcast_to(row, (BLK, Q_TILE))
        return c

    lax.fori_loop(0, nfar, mask_body, 0)

    causal = (kp_iota - Q_TILE) <= t_iota
    in_seq = (kp_iota + qs) >= Q_TILE
    for i in range(4):
        bi = 2 * qb - 2 + i
        row = selr_ref[pl.ds(jnp.maximum(bi, 0), 1), :]
        ok = (row > 0.5) & causal[i * BLK:(i + 1) * BLK] & (bi >= 0)
        nearm_ref[i * BLK:(i + 1) * BLK, :] = jnp.where(ok, 0.0, NEG_INF)
    win_near = jnp.where(causal & in_seq, 0.0, NEG_INF)
    nwf = WINDOW - Q_TILE
    wf_iota = lax.broadcasted_iota(I32, (nwf, 1), 0)
    win_far = jnp.where((wf_iota >= t_iota) & ((wf_iota + qs) >= WINDOW), 0.0, NEG_INF)

    _flash_reset(m_ref, l_ref, a_ref)
    heads = range(H_A)
    q_of = lambda h: q_ref[:, h * DH_A:(h + 1) * DH_A]

    def far_body(j, c):
        r0 = pl.multiple_of(Q_TILE + j * K_TILE, Q_TILE)
        kt = ks_ref[pl.ds(r0, K_TILE), :]
        vt = vst_ref[:, pl.ds(r0, K_TILE)]
        mt = madd_ref[pl.ds(pl.multiple_of(j * K_TILE, K_TILE), K_TILE), :]
        _flash_tile(m_ref, l_ref, a_ref, [_dot_nt(kt, q_of(h)) + mt for h in heads], lambda h: vt)
        return c

    lax.fori_loop(0, nfar, far_body, 0)
    near0 = pl.multiple_of(qs, Q_TILE)
    kt = ks_ref[pl.ds(near0, 2 * Q_TILE), :]
    vt = vst_ref[:, pl.ds(near0, 2 * Q_TILE)]
    nearm = nearm_ref[...]
    _flash_tile(m_ref, l_ref, a_ref, [_dot_nt(kt, q_of(h)) + (band_ref[h] + nearm) for h in heads], lambda h: vt)
    for h in heads:
        out_ref[h] += gt_ref[3 * h + 1:3 * h + 2, :] * (a_ref[h] / jnp.maximum(l_ref[h], 1e-30))

    _flash_reset(m_ref, l_ref, a_ref)
    kt = kw_ref[pl.ds(near0, nwf), :]
    vt = vwt_ref[:, pl.ds(near0, nwf)]
    _flash_tile(m_ref, l_ref, a_ref, [_dot_nt(kt, q_of(h)) + win_far for h in heads], lambda h: vt)
    wn0 = pl.multiple_of(qs + nwf, Q_TILE)
    kt2 = kw_ref[pl.ds(wn0, 2 * Q_TILE), :]
    vt2 = vwt_ref[:, pl.ds(wn0, 2 * Q_TILE)]
    _flash_tile(m_ref, l_ref, a_ref, [_dot_nt(kt2, q_of(h)) + (band_ref[h] + win_near) for h in heads], lambda h: vt2)
    for h in heads:
        tot = out_ref[h] + gt_ref[3 * h + 2:3 * h + 3, :] * (a_ref[h] / jnp.maximum(l_ref[h], 1e-30))
        o_ref[:, h * DH_A:(h + 1) * DH_A] = tot.T.astype(BF16)


def _nsa_prompt(rel_table, qn, gates_t, ck, cvt, ks_p, vst_p, kw_p, vwt_p, n_seq, seq):
    nb = seq // BLK
    nq = seq // Q_TILE
    k_sel = min(N_SEL, nb)
    per_seq = lambda a: pl.BlockSpec((None,) + a.shape[1:], lambda n, i: (n,) + (0,) * (a.ndim - 1),
                                     pipeline_mode=pl.Buffered(1))
    kern = functools.partial(_nsa_prompt_kernel, nb=nb, k_sel=k_sel)
    stat = pltpu.VMEM((H_A, 1, Q_TILE), F32)
    return pl.pallas_call(
        kern,
        grid=(n_seq, nq),
        in_specs=[_smem_spec(),
                  pl.BlockSpec((Q_TILE, H_A * DH_A), lambda n, i: (n * nq + i, 0)),
                  pl.BlockSpec((None, 3 * H_A, Q_TILE), lambda n, i: (n, 0, i)),
                  per_seq(ck), per_seq(cvt), per_seq(ks_p), per_seq(vst_p), per_seq(kw_p), per_seq(vwt_p)],
        out_specs=pl.BlockSpec((Q_TILE, H_A * DH_A), lambda n, i: (n * nq + i, 0)),
        out_shape=jax.ShapeDtypeStruct((n_seq * seq, H_A * DH_A), BF16),
        scratch_shapes=[pltpu.VMEM((H_A, DH_A, Q_TILE), F32), stat, stat, pltpu.VMEM((H_A, DH_A, Q_TILE), F32),
                        pltpu.VMEM((H_A, 2 * Q_TILE, Q_TILE), F32), pltpu.VMEM((H_A, 8, Q_TILE), F32),
                        pltpu.VMEM((nb, Q_TILE), F32), pltpu.VMEM((nb, Q_TILE), F32), pltpu.VMEM((nb, Q_TILE), F32),
                        pltpu.VMEM((seq, Q_TILE), F32), pltpu.VMEM((2 * Q_TILE, Q_TILE), F32)],
        compiler_params=_cparams(("arbitrary", "arbitrary")),
        name="nsa_prompt",
    )(rel_table, qn, gates_t, ck, cvt, ks_p, vst_p, kw_p, vwt_p)


def _mla_prompt_kernel(qn_ref, qr_ref, kn_ref, kr_ref, vt_ref, o_ref, m_ref, l_ref, a_ref, *, tk):
    qb = pl.program_id(1)
    qs = qb * Q_TILE
    nfull = qs // tk
    t_iota = lax.broadcasted_iota(I32, (1, Q_TILE), 1)
    k_iota = lax.broadcasted_iota(I32, (tk, 1), 0)
    diag0 = pl.multiple_of(nfull * tk, tk)
    diag_mask = jnp.where((k_iota + diag0) <= (t_iota + qs), 0.0, NEG_INF)
    _flash_reset(m_ref, l_ref, a_ref)

    def tile(r0, mask):
        kr = kr_ref[pl.ds(r0, tk), :]
        logits = []
        for h in range(H_B):
            kh = jnp.concatenate([kn_ref[pl.ds(r0, tk), h * D_NOPE:(h + 1) * D_NOPE], kr], axis=1)
            qh = jnp.concatenate([qn_ref[:, h * D_NOPE:(h + 1) * D_NOPE], qr_ref[:, h * LANES:(h + 1) * LANES]], axis=1)
            s = _dot_nt(kh, qh)
            logits.append(s if mask is None else s + mask)
        _flash_tile(m_ref, l_ref, a_ref, logits, lambda h: vt_ref[h * D_V:(h + 1) * D_V, pl.ds(r0, tk)])

    def body(j, c):
        tile(pl.multiple_of(j * tk, tk), None)
        return c

    lax.fori_loop(0, nfull, body, 0)
    tile(diag0, diag_mask)
    for h in range(H_B):
        o_ref[:, h * D_V:(h + 1) * D_V] = (a_ref[h] / jnp.maximum(l_ref[h], 1e-30)).T.astype(BF16)


def _mla_prompt(qnope, qrope, knope, krp, vt, n_seq, seq):
    nq = seq // Q_TILE
    tk = min(K_TILE, seq)
    per_seq = lambda a: pl.BlockSpec((None,) + a.shape[1:], lambda n, i: (n,) + (0,) * (a.ndim - 1),
                                     pipeline_mode=pl.Buffered(1))
    qspec = lambda w: pl.BlockSpec((Q_TILE, w), lambda n, i: (n * nq + i, 0))
    stat = pltpu.VMEM((H_B, 1, Q_TILE), F32)
    return pl.pallas_call(
        functools.partial(_mla_prompt_kernel, tk=tk),
        grid=(n_seq, nq),
        in_specs=[qspec(H_B * D_NOPE), qspec(H_B * LANES), per_seq(knope), per_seq(krp), per_seq(vt)],
        out_specs=qspec(H_B * D_V),
        out_shape=jax.ShapeDtypeStruct((n_seq * seq, H_B * D_V), BF16),
        scratch_shapes=[stat, stat, pltpu.VMEM((H_B, D_V, Q_TILE), F32)],
        compiler_params=_cparams(("arbitrary", "arbitrary")),
        name="mla_prompt",
    )(qnope, qrope, knope, krp, vt)


def _outproj_kernel(oa_ref, ob_ref, w_ref, x_ref, gt_ref, sc_ref, sh_ref, g_ref, x1_ref, h_ref):
    half = oa_ref.shape[1]
    mixed = _dot(oa_ref[...], w_ref[0:half, :]) + _dot(ob_ref[...], w_ref[half:2 * half, :])
    x1 = x_ref[...] + gt_ref[...] * mixed
    x1_ref[...] = x1
    h_ref[...] = (_rms(x1, g_ref[...]) * (1.0 + sc_ref[...]) + sh_ref[...]).astype(BF16)


def _outproj(oa, ob, w_out, x2d, gt, sc, sh, g_ffn, rows_per_group):
    rows, d = x2d.shape
    tm = min(512, rows)
    row = lambda n: pl.BlockSpec((tm, n), lambda i: (i, 0))
    ms = _mod_spec(gt, tm, rows_per_group)
    return pl.pallas_call(
        _outproj_kernel,
        grid=(rows // tm,),
        in_specs=[row(oa.shape[1]), row(ob.shape[1]), _const_spec(w_out.shape), row(d), ms, ms, ms,
                  _const_spec((1, d))],
        out_specs=[row(d), row(d)],
        out_shape=[jax.ShapeDtypeStruct((rows, d), F32), jax.ShapeDtypeStruct((rows, d), BF16)],
        compiler_params=_cparams(("arbitrary",)),
        name="outproj",
    )(oa, ob, w_out, x2d, gt, sc, sh, g_ffn.reshape(1, d))


def _ffn_kernel(h_ref, wg_ref, wu_ref, wd_ref, x1_ref, gt_ref, g_ref, y_ref, acc_ref):
    j = pl.program_id(1)
    h = h_ref[...]
    a = _dot(h, wg_ref[...])
    u = _dot(h, wu_ref[...])
    part = _dot(((a * jax.nn.sigmoid(a)) * u).astype(BF16), wd_ref[...])

    @pl.when(j == 0)
    def _():
        acc_ref[...] = part

    @pl.when(j > 0)
    def _():
        acc_ref[...] += part

    @pl.when(j == pl.num_programs(1) - 1)
    def _():
        x2 = x1_ref[...] + gt_ref[...] * acc_ref[...]
        y_ref[...] = _rms(x2, g_ref[...])


def _ffn(h2, wg, wu, wd, x1, gt, g_final, rows_per_group):
    rows, d = x1.shape
    dff = wg.shape[1]
    tm = min(512, rows)
    tf = 512
    row = lambda n: pl.BlockSpec((tm, n), lambda i, j: (i, 0))
    return pl.pallas_call(
        _ffn_kernel,
        grid=(rows // tm, dff // tf),
        in_specs=[row(d), pl.BlockSpec((d, tf), lambda i, j: (0, j)), pl.BlockSpec((d, tf), lambda i, j: (0, j)),
                  pl.BlockSpec((tf, d), lambda i, j: (j, 0)), row(d), _mod_spec(gt, tm, rows_per_group),
                  pl.BlockSpec((1, d), lambda i, j: (0, 0))],
        out_specs=row(d),
        out_shape=jax.ShapeDtypeStruct((rows, d), F32),
        scratch_shapes=[pltpu.VMEM((tm, d), F32)],
        compiler_params=_cparams(("arbitrary", "arbitrary")),
        name="ffn",
    )(h2, wg, wu, wd, x1, gt, g_final.reshape(1, d))


def _headproj_kernel(x_ref, w_ref, o_ref):
    o_ref[...] = _dot(x_ref[...].astype(BF16), w_ref[...])


def _headproj(x, w):
    rows = x.shape[0]
    nh, k, n = w.shape
    return pl.pallas_call(
        _headproj_kernel,
        grid=(nh,),
        in_specs=[pl.BlockSpec((rows, k), lambda h: (0, h)), pl.BlockSpec((None, k, n), lambda h: (h, 0, 0))],
        out_specs=pl.BlockSpec((rows, n), lambda h: (0, h)),
        out_shape=jax.ShapeDtypeStruct((rows, nh * n), F32),
        compiler_params=_cparams(("arbitrary",)),
        name="headproj",
    )(x, w)


def _chunk_page_copy(cache_ref, pt_ref, buf_ref, sem_ref, step, nchunk, cpages, ring, i):
    seq = step // nchunk
    c = step % nchunk
    slot = step % ring
    page = pt_ref[seq, c * cpages + i]
    return pltpu.make_async_copy(cache_ref.at[pl.ds(pl.multiple_of(page * PAGE_ROWS, PAGE_ROWS), PAGE_ROWS), :],
                                 buf_ref.at[slot, pl.ds(pl.multiple_of(i * PAGE_ROWS, PAGE_ROWS), PAGE_ROWS), :],
                                 sem_ref.at[slot])


def _ring_schedule(start_chunk, s, nstep, ring):
    ahead = ring - 1

    @pl.when(s == 0)
    def _():
        for d in range(ahead):
            @pl.when(d < nstep)
            def _():
                start_chunk(d)

    @pl.when(s + ahead < nstep)
    def _():
        start_chunk(s + ahead)


def _cmp_sample_kernel(pt_ref, cache_ref, q_ref, w1k_ref, w2k_ref, w1v_ref, w2v_ref, tbl_ref,
                       ocmp_ref, score_ref, buf_ref, sem_ref, stage_ref, *, nchunk, cpages, nbk, pitch, past):
    s = pl.program_id(0)
    nstep = pl.num_programs(0)
    c = s % nchunk
    slot = s % CMP_RING

    def start_chunk(step):
        def body(i, cc):
            _chunk_page_copy(cache_ref, pt_ref, buf_ref, sem_ref, step, nchunk, cpages, CMP_RING, i).start()
            return cc
        lax.fori_loop(0, cpages, body, 0, unroll=DMA_UNROLL)

    _ring_schedule(start_chunk, s, nstep, CMP_RING)

    def wait_body(i, cc):
        _chunk_page_copy(cache_ref, pt_ref, buf_ref, sem_ref, s, nchunk, cpages, CMP_RING, i).wait()
        return cc
    lax.fori_loop(0, cpages, wait_body, 0, unroll=DMA_UNROLL)

    for i in range(cpages):
        for half in range(2):
            blk = (c * cpages + i) * 2 + half
            for comp in range(2):
                slab = buf_ref[slot, pl.ds(i * PAGE_ROWS + half * BLK_ROWS + comp, BLK, stride=NSA_ROWS), :]
                stage_ref[comp, pl.ds(blk, BLK, stride=pitch), :] = slab

    @pl.when(c == nchunk - 1)
    def _():
        group = 16

        def compress(comp, w1_ref, w2_ref):
            pre = jnp.zeros((nbk, DH_A), F32)
            for g in range(BLK // group):
                x = jnp.concatenate([stage_ref[comp, pl.ds((g * group + j) * pitch, nbk), :] for j in range(group)],
                                    axis=1).astype(BF16)
                pre = pre + _dot(x, w1_ref[g * group * DH_A:(g + 1) * group * DH_A, :])
            return _dot(jax.nn.gelu(pre).astype(BF16), w2_ref[...]).astype(BF16)

        ck = compress(0, w1k_ref, w2k_ref)
        cv = compress(1, w1v_ref, w2v_ref)
        q = q_ref[...]
        rows = q.shape[0]
        nt = rows // H_A
        t_of_row = lax.broadcasted_iota(I32, (rows, 1), 0) // H_A
        b_iota = lax.broadcasted_iota(I32, (1, nbk), 1)
        n_cmp = jnp.maximum(past + t_of_row - (b_iota * BLK + (BLK - 1)), 0)
        lg = _dot_nt(q, ck) + _bias_from_cols(n_cmp, tbl_ref)
        m = jnp.max(lg, axis=-1, keepdims=True)
        p = jnp.exp2(lg - m)
        p = p / jnp.maximum(jnp.sum(p, axis=-1, keepdims=True), 1e-30)
        ocmp_ref[...] = _dot(p.astype(BF16), cv)

        width = score_ref.shape[-1]
        bid = lax.broadcasted_iota(I32, (1, width), 1)
        imp = jnp.concatenate([jnp.sum(p[t * H_A:(t + 1) * H_A], axis=0, keepdims=True) for t in range(nt)], axis=0)
        imp = jnp.concatenate([imp, jnp.zeros((nt, width - nbk), F32)], axis=1)
        cur = nbk
        forced = (bid == 0) | (bid == cur) | (bid == cur - 1)
        score = jnp.where(forced, FORCED_SCORE, imp)
        score_ref[...] = jnp.where(bid > cur, NEG_INF, score)


def _select_kernel(s_ref, sel_ref, *, nbs, k_sel):
    s = s_ref[...]
    b_iota = lax.broadcasted_iota(I32, (s.shape[0], 1), 0)

    def rank_body(bp, rank):
        row = s_ref[pl.ds(bp, 1), :]
        tie = jnp.where(bp < b_iota, 1, 0)
        return rank + jnp.where(row > s, 1, jnp.where(row == s, tie, 0))

    rank = lax.fori_loop(0, nbs, rank_body, jnp.zeros(s.shape, I32))
    rows = [jnp.sum(jnp.where(rank == r, b_iota, 0), axis=0, keepdims=True) for r in range(k_sel)]
    rows += [jnp.zeros_like(rows[0])] * (sel_ref.shape[0] - k_sel)
    sel_ref[...] = jnp.concatenate(rows, axis=0)


def _select_blocks(score_t, nbs, k_sel):
    width, rows = score_t.shape
    return pl.pallas_call(
        functools.partial(_select_kernel, nbs=nbs, k_sel=k_sel),
        grid=(rows // LANES,),
        in_specs=[pl.BlockSpec((width, LANES), lambda i: (0, i))],
        out_specs=pl.BlockSpec((N_SEL, LANES), lambda i: (0, i)),
        out_shape=jax.ShapeDtypeStruct((N_SEL, rows), I32),
        compiler_params=_cparams(("arbitrary",)),
        name="select_blocks",
    )(score_t)


def _cmp_sample(page_table, cache2d, q32, w1k, w2k, w1v, w2v, tbl_rows, past):
    nseq, npages = page_table.shape
    rows = q32.shape[1]
    nt = rows // H_A
    nbk = past // BLK
    width = -(-(nbk + 1) // LANES) * LANES
    cpages = min(CHUNK_PAGES, npages)
    nchunk = npages // cpages
    pitch = nbk + 8
    kern = functools.partial(_cmp_sample_kernel, nchunk=nchunk, cpages=cpages, nbk=nbk, pitch=pitch, past=past)
    cst = lambda a: pl.BlockSpec(a.shape, lambda s, pt: (0,) * a.ndim, pipeline_mode=pl.Buffered(1))
    per = lambda a: pl.BlockSpec((None,) + a.shape[1:], lambda s, pt: (s // nchunk,) + (0,) * (a.ndim - 1))
    grid_spec = pltpu.PrefetchScalarGridSpec(
        num_scalar_prefetch=1,
        grid=(nseq * nchunk,),
        in_specs=[pl.BlockSpec(memory_space=pl.ANY), per(q32), cst(w1k), cst(w2k), cst(w1v), cst(w2v), cst(tbl_rows)],
        out_specs=[pl.BlockSpec((None, rows, DH_A), lambda s, pt: (s // nchunk, 0, 0)),
                   pl.BlockSpec((None, nt, width), lambda s, pt: (s // nchunk, 0, 0))],
        scratch_shapes=[pltpu.VMEM((CMP_RING, cpages * PAGE_ROWS, DH_A), F32), pltpu.SemaphoreType.DMA((CMP_RING,)),
                        pltpu.VMEM((2, BLK * pitch, DH_A), F32)],
    )
    return pl.pallas_call(
        kern,
        grid_spec=grid_spec,
        out_shape=[jax.ShapeDtypeStruct((nseq, rows, DH_A), F32), jax.ShapeDtypeStruct((nseq, nt, width), F32)],
        compiler_params=_cparams(("arbitrary",)),
        name="cmp_sample",
    )(page_table, cache2d, q32, w1k, w2k, w1v, w2v, tbl_rows)


def _slc_copy(cache_ref, pt_ref, sel_ref, buf_ref, sem_ref, seq, slot, i, nbk):
    blk = jnp.minimum(sel_ref[seq, i], nbk - 1)
    r0 = pl.multiple_of(pt_ref[seq, blk // 2] * PAGE_ROWS + (blk % 2) * BLK_ROWS, BLK_ROWS)
    return pltpu.make_async_copy(cache_ref.at[pl.ds(r0, BLK_ROWS), :], buf_ref.at[slot, i], sem_ref.at[slot])


def _slc_sample_kernel(pt_ref, sel_ref, cache_ref, q_ref, rel_ref, new_ref, state_ref, wnew_ref, tbl_ref,
                       ocmp_ref, g0_ref, g1_ref, g2_ref, o_ref, buf_ref, sem_ref, *, nbk, nslot, nt, wb):
    b = pl.program_id(0)
    nseq = pl.num_programs(0)
    slot = b % 2
    ntot = nt * nslot

    def start_all(seq, sl):
        def body(i, c):
            @pl.when(sel_ref[seq, i] < nbk)
            def _():
                _slc_copy(cache_ref, pt_ref, sel_ref, buf_ref, sem_ref, seq, sl, i, nbk).start()
            return c
        lax.fori_loop(0, ntot, body, 0, unroll=DMA_UNROLL)

    @pl.when(b == 0)
    def _():
        start_all(0, 0)

    @pl.when(b + 1 < nseq)
    def _():
        start_all(b + 1, 1 - slot)

    def wait_body(i, c):
        @pl.when(sel_ref[b, i] < nbk)
        def _():
            _slc_copy(cache_ref, pt_ref, sel_ref, buf_ref, sem_ref, b, slot, i, nbk).wait()

        @pl.when(sel_ref[b, i] >= nbk)
        def _():
            buf_ref[slot, i] = new_ref[...]
        return c
    lax.fori_loop(0, ntot, wait_body, 0, unroll=DMA_UNROLL)

    kst = state_ref[pl.ds(0, wb, stride=2), :].astype(BF16)
    vst = state_ref[pl.ds(1, wb, stride=2), :].astype(BF16)
    nwn = wnew_ref.shape[0] // 2
    kwn = wnew_ref[pl.ds(0, nwn, stride=2), :].astype(BF16)
    vwn = wnew_ref[pl.ds(1, nwn, stride=2), :].astype(BF16)
    st_iota = lax.broadcasted_iota(I32, (1, wb), 1)
    nw_iota = lax.broadcasted_iota(I32, (1, nwn), 1)
    for t in range(nt):
        q = q_ref[t * H_A:(t + 1) * H_A, :]
        blocks = buf_ref.at[slot]
        ks = jnp.concatenate([blocks[t * nslot + r, pl.ds(2, BLK, stride=NSA_ROWS), :] for r in range(nslot)], axis=0)
        vs = jnp.concatenate([blocks[t * nslot + r, pl.ds(3, BLK, stride=NSA_ROWS), :] for r in range(nslot)], axis=0)
        rel = rel_ref[t:t + 1, :]
        lg = _dot_nt(q, ks.astype(BF16)) + _bias_from_cols(jnp.maximum(rel, 0), tbl_ref)
        ok = rel >= 0
        lg = jnp.where(ok, lg, NEG_INF)
        m = jnp.max(lg, axis=-1, keepdims=True)
        p = jnp.exp2(lg - m) * ok.astype(F32)
        p = p / jnp.maximum(jnp.sum(p, axis=-1, keepdims=True), 1e-30)
        o_slc = _dot(p.astype(BF16), vs.astype(BF16))

        rel_s = wb + t - st_iota
        rel_n = t - nw_iota
        ls = _dot_nt(q, kst) + _bias_from_cols(rel_s, tbl_ref)
        ln = _dot_nt(q, kwn) + _bias_from_cols(jnp.maximum(rel_n, 0), tbl_ref)
        ok_s = rel_s <= WINDOW
        ok_n = rel_n >= 0
        ls = jnp.where(ok_s, ls, NEG_INF)
        ln = jnp.where(ok_n, ln, NEG_INF)
        m = jnp.maximum(jnp.max(ls, axis=-1, keepdims=True), jnp.max(ln, axis=-1, keepdims=True))
        ps = jnp.exp2(ls - m) * ok_s.astype(F32)
        pn = jnp.exp2(ln - m) * ok_n.astype(F32)
        den = jnp.maximum(jnp.sum(ps, axis=-1, keepdims=True) + jnp.sum(pn, axis=-1, keepdims=True), 1e-30)
        o_win = _dot((ps / den).astype(BF16), vst) + _dot((pn / den).astype(BF16), vwn)
        r = slice(t * H_A, (t + 1) * H_A)
        o_ref[r, :] = g0_ref[r, :] * ocmp_ref[r, :] + g1_ref[r, :] * o_slc + g2_ref[r, :] * o_win


def _slc_sample(page_table, sel_flat, cache2d, q32, rel, new_blk, state, wnew, tbl_t, ocmp, g0, g1, g2, nbk, nslot):
    nseq = page_table.shape[0]
    rows = q32.shape[1]
    nt = rows // H_A
    wb = state.shape[1] // 2
    per = lambda a: pl.BlockSpec((None,) + a.shape[1:], lambda b, *_: (b,) + (0,) * (a.ndim - 1))
    kern = functools.partial(_slc_sample_kernel, nbk=nbk, nslot=nslot, nt=nt, wb=wb)
    grid_spec = pltpu.PrefetchScalarGridSpec(
        num_scalar_prefetch=2,
        grid=(nseq,),
        in_specs=[pl.BlockSpec(memory_space=pl.ANY), per(q32), per(rel), per(new_blk), per(state), per(wnew),
                  pl.BlockSpec(tbl_t.shape, lambda b, *_: (0, 0)), per(ocmp), per(g0), per(g1), per(g2)],
        out_specs=pl.BlockSpec((None, rows, DH_A), lambda b, *_: (b, 0, 0)),
        scratch_shapes=[pltpu.VMEM((2, nt * nslot, BLK_ROWS, DH_A), F32), pltpu.SemaphoreType.DMA((2,))],
    )
    return pl.pallas_call(
        kern,
        grid_spec=grid_spec,
        out_shape=jax.ShapeDtypeStruct((nseq, rows, DH_A), F32),
        compiler_params=_cparams(("arbitrary",)),
        name="slc_sample",
    )(page_table, sel_flat, cache2d, q32, rel, new_blk, state, wnew, tbl_t, ocmp, g0, g1, g2)


def _mla_page_copy(cache_ref, pt_ref, buf_ref, sem_ref, step, nchunk, cpages, i):
    seq = step // nchunk
    c = step % nchunk
    slot = step % MLA_RING
    page = pt_ref[seq, c * cpages + i]
    return pltpu.make_async_copy(cache_ref.at[page],
                                 buf_ref.at[slot, :, pl.ds(pl.multiple_of(i * PAGE_SIZE, PAGE_SIZE), PAGE_SIZE)],
                                 sem_ref.at[slot])


def _mla_sample_kernel(pt_ref, cache_ref, q_ref, new_ref, o_ref, buf_ref, sem_ref, m_ref, l_ref, acc_ref,
                       *, nchunk, cpages):
    s = pl.program_id(0)
    nstep = pl.num_programs(0)
    c = s % nchunk
    slot = s % MLA_RING

    def start_chunk(step):
        def body(i, cc):
            _mla_page_copy(cache_ref, pt_ref, buf_ref, sem_ref, step, nchunk, cpages, i).start()
            return cc
        lax.fori_loop(0, cpages, body, 0, unroll=DMA_UNROLL)

    _ring_schedule(start_chunk, s, nstep, MLA_RING)
    q = q_ref[...]

    @pl.when(c == 0)
    def _():
        new = new_ref[...]
        sn = _dot_nt(q, new.astype(BF16))
        rows = sn.shape[0]
        t_of_row = lax.broadcasted_iota(I32, (rows, 1), 0) // H_B
        j = lax.broadcasted_iota(I32, (1, new.shape[0]), 1)
        ok = j <= t_of_row
        sn = jnp.where(ok, sn, NEG_INF)
        m = jnp.max(sn, axis=-1, keepdims=True)
        p = jnp.exp2(sn - m) * ok.astype(F32)
        m_ref[...] = m
        l_ref[...] = jnp.sum(p, axis=-1, keepdims=True)
        acc_ref[...] = _dot(p.astype(BF16), new[:, 0:KV_LORA].astype(BF16))

    def wait_body(i, cc):
        _mla_page_copy(cache_ref, pt_ref, buf_ref, sem_ref, s, nchunk, cpages, i).wait()
        return cc
    lax.fori_loop(0, cpages, wait_body, 0, unroll=DMA_UNROLL)

    kv = buf_ref[slot].astype(BF16)
    sc = _dot(q, kv)
    m_old = m_ref[...]
    m_new = jnp.maximum(m_old, jnp.max(sc, axis=-1, keepdims=True))
    alpha = jnp.exp2(m_old - m_new)
    p = jnp.exp2(sc - m_new)
    l_new = alpha * l_ref[...] + jnp.sum(p, axis=-1, keepdims=True)
    acc = alpha * acc_ref[...] + _dot_nt(p.astype(BF16), kv[0:KV_LORA, :])
    m_ref[...] = m_new
    l_ref[...] = l_new
    acc_ref[...] = acc

    @pl.when(c == nchunk - 1)
    def _():
        o_ref[...] = acc / jnp.maximum(l_new, 1e-30)


def _mla_sample(page_table, cache_t, q576, new_rows):
    nseq, npages = page_table.shape
    rows = q576.shape[1]
    cpages = min(CHUNK_PAGES, npages)
    nchunk = npages // cpages
    kern = functools.partial(_mla_sample_kernel, nchunk=nchunk, cpages=cpages)
    per = lambda a: pl.BlockSpec((None,) + a.shape[1:], lambda s, pt: (s // nchunk,) + (0,) * (a.ndim - 1))
    grid_spec = pltpu.PrefetchScalarGridSpec(
        num_scalar_prefetch=1,
        grid=(nseq * nchunk,),
        in_specs=[pl.BlockSpec(memory_space=pl.ANY), per(q576), per(new_rows)],
        out_specs=pl.BlockSpec((None, rows, KV_LORA), lambda s, pt: (s // nchunk, 0, 0)),
        scratch_shapes=[pltpu.VMEM((MLA_RING, MLA_ROW, cpages * PAGE_SIZE), F32), pltpu.SemaphoreType.DMA((MLA_RING,)),
                        pltpu.VMEM((rows, 1), F32), pltpu.VMEM((rows, 1), F32), pltpu.VMEM((rows, KV_LORA), F32)],
    )
    return pl.pallas_call(
        kern,
        grid_spec=grid_spec,
        out_shape=jax.ShapeDtypeStruct((nseq, rows, KV_LORA), F32),
        compiler_params=_cparams(("arbitrary",)),
        name="mla_sample",
    )(page_table, cache_t, q576, new_rows)


def _prep_w_in(w_in):
    d = w_in.shape[0]
    sizes = (H_A * DH_A, 4 * DH_A, 2 * DH_A, 3 * H_A, Q_LORA, KV_LORA, D_ROPE)
    offs = np.concatenate([[0], np.cumsum(sizes)])
    seg = lambda i: w_in[:, int(offs[i]):int(offs[i + 1])]
    zq, zkv, zwin, zg, zcq, zckv, zkr = (seg(i) for i in range(7))
    half = D_ROPE // 2
    zrot = jnp.concatenate([-zkr[:, half:], zkr[:, :half]], axis=1)
    pad = lambda a: jnp.pad(a, ((0, 0), (0, LANES - a.shape[1])))
    w_ext = jnp.concatenate([zq, zkv, zwin, zcq, zckv, pad(zkr), pad(zrot), pad(zg)], axis=1)
    assert w_ext.shape == (d, C_END)
    return w_ext.astype(BF16)


def _prep_w_uq(w_uq):
    r = w_uq.shape[0]
    w = w_uq.reshape(r, H_B, D_NOPE + D_ROPE)
    nope = w[:, :, :D_NOPE].reshape(r, H_B * D_NOPE)
    rope = w[:, :, D_NOPE:]
    half = D_ROPE // 2
    rot = jnp.concatenate([-rope[:, :, half:], rope[:, :, :half]], axis=2)
    padh = lambda a: jnp.pad(a, ((0, 0), (0, 0), (0, LANES - D_ROPE))).reshape(r, H_B * LANES)
    return jnp.concatenate([nope, padh(rope), padh(rot)], axis=1).astype(BF16)


def _rope_tables(pos):
    half = D_ROPE // 2
    inv = ROPE_THETA ** (-jnp.arange(half, dtype=F32) / half)
    ang = pos.astype(F32)[:, None] * inv[None, :]
    cos = jnp.cos(ang)
    sin = jnp.sin(ang)
    z = jnp.zeros((pos.shape[0], LANES - D_ROPE), F32)
    return jnp.concatenate([cos, cos, z], axis=1), jnp.concatenate([sin, sin, z], axis=1)


def kernel(x_prompt, x_sample, cache_nsa_kv, cache_mla, state_win_kv, page_table, c_prompt, c_sample, rel_table,
           g_attn, g_ffn, g_final, w_ada, b_ada, w_in, g_qnorm, w_uq, g_kvnorm, w_uk, w_uv, w_ck1, w_ck2, w_cv1,
           w_cv2, w_out, w_gate, w_up, w_down):
    n_p, seq, d = x_prompt.shape
    n_s, t_s, _ = x_sample.shape
    npages = page_table.shape[1]
    past = npages * PAGE_SIZE
    wb = state_win_kv.shape[1]
    n_pool = cache_nsa_kv.shape[0]
    assert seq % K_TILE == 0 and seq >= WINDOW and wb == WINDOW and t_s <= 8
    assert cache_nsa_kv.shape[1:] == (PAGE_SIZE, NSA_ROWS, DH_A) and cache_mla.shape[1:] == (PAGE_SIZE, MLA_ROW)

    w_ext = _prep_w_in(w_in)
    wuq_ext = _prep_w_uq(w_uq)
    wkv = jnp.concatenate([w_uk.reshape(KV_LORA, H_B * D_NOPE), w_uv.reshape(KV_LORA, H_B * D_V)], axis=1).astype(BF16)
    w_out_b = w_out.astype(BF16)
    wg_b, wu_b, wd_b = w_gate.astype(BF16), w_up.astype(BF16), w_down.astype(BF16)
    w1k_b, w2k_b, w1v_b, w2v_b = (w.astype(BF16) for w in (w_ck1, w_ck2, w_cv1, w_cv2))

    n_c = n_p + n_s
    c_all = jnp.pad(jnp.concatenate([c_prompt, c_sample], axis=0), ((0, (-n_c) % 8), (0, 0)))
    mod = _modulation(c_all, w_ada, b_ada)
    mod_p = mod[:n_p].reshape(n_p, 6, 1, d)
    mod_s = jnp.repeat(mod[n_p:n_c].reshape(n_s, 6, d), t_s, axis=0).transpose(1, 0, 2)[:, None]
    sh_a, sc_a, gt_a, sh_f, sc_f, gt_f = (mod_p[:, i] for i in range(6))
    sh_a_s, sc_a_s, gt_a_s, sh_f_s, sc_f_s, gt_f_s = (mod_s[i] for i in range(6))

    cos_p, sin_p = _rope_tables(jnp.arange(seq, dtype=I32))
    pos_s = past + jnp.tile(jnp.arange(t_s, dtype=I32), n_s)
    cos_s, sin_s = _rope_tables(pos_s)
    xp2 = x_prompt.reshape(n_p * seq, d)
    xs2 = x_sample.reshape(n_s * t_s, d)
    (qn_p, nsa_p, win_p, gates_p, mla_p, qnope_p, qrope_p, krp_p, knope_p, v_p, kvb_p) = _inproj(
        xp2, sc_a, sh_a, g_attn, w_ext, g_qnorm, wuq_ext, g_kvnorm, wkv, cos_p, sin_p, seq)
    (qn_s, nsa_s, win_s, gates_s, mla_s, qnope_s, qrope_s, _krp_s, _kn_s, _v_s, _kvb_s) = _inproj(
        xs2, sc_a_s, sh_a_s, g_attn, w_ext, g_qnorm, wuq_ext, g_kvnorm, wkv, cos_s, sin_s, t_s)

    nb = seq // BLK
    kvb_p3 = kvb_p.reshape(n_p, seq, 6 * DH_A)
    part = lambda i: kvb_p3[:, :, i * DH_A:(i + 1) * DH_A]
    kb = part(0).reshape(n_p * nb, BLK * DH_A)
    vb = part(1).reshape(n_p * nb, BLK * DH_A)
    ck_p, cv_p = _compress_prompt(kb, vb, w1k_b, w2k_b, w1v_b, w2v_b)
    ck_p = ck_p.reshape(n_p, nb, DH_A).astype(BF16)
    cvt_p = cv_p.reshape(n_p, nb, DH_A).transpose(0, 2, 1).astype(BF16)
    gates_t = gates_p[:, :3 * H_A].reshape(n_p, seq, 3 * H_A).transpose(0, 2, 1)

    ks, vs, kw, vw = part(2), part(3), part(4), part(5)
    ks_p = jnp.pad(ks, ((0, 0), (Q_TILE, 0), (0, 0)))
    vst_p = jnp.pad(vs, ((0, 0), (Q_TILE, 0), (0, 0))).transpose(0, 2, 1)
    kw_p = jnp.pad(kw, ((0, 0), (WINDOW, 0), (0, 0)))
    vwt_p = jnp.pad(vw, ((0, 0), (WINDOW, 0), (0, 0))).transpose(0, 2, 1)
    o_nsa_p = _nsa_prompt(rel_table, qn_p, gates_t, ck_p, cvt_p, ks_p, vst_p, kw_p, vwt_p, n_p, seq)

    o_mla_p = _mla_prompt(qnope_p, qrope_p, knope_p.reshape(n_p, seq, H_B * D_NOPE), krp_p.reshape(n_p, seq, LANES),
                          v_p.reshape(n_p, seq, H_B * D_V).transpose(0, 2, 1), n_p, seq)

    rows_s = t_s * H_A
    nbk = past // BLK
    nbs = -(-(past + t_s) // BLK)
    assert nbs == nbk + 1
    k_sel = min(N_SEL, nbs)
    cache2d = cache_nsa_kv.reshape(n_pool * PAGE_ROWS, DH_A)
    cache_t = cache_mla.transpose(0, 2, 1)
    q32 = qn_s.reshape(n_s, rows_s, DH_A)
    tbl_t = rel_table.T
    tbl_rows = jnp.tile(tbl_t, (t_s, 1))

    ocmp_s, score = _cmp_sample(page_table, cache2d, q32, w1k_b, w2k_b, w1v_b, w2v_b, tbl_rows, past)
    n_q = n_s * t_s
    score_t = jnp.pad(score.reshape(n_q, -1), ((0, (-n_q) % LANES), (0, 0))).T
    sel = _select_blocks(score_t, nbs, k_sel).T[:n_q, :k_sel].reshape(n_s, t_s, k_sel)

    qpos = past + jnp.arange(t_s, dtype=I32)
    tok = sel[..., None] * BLK + jnp.arange(BLK, dtype=I32)
    rel_slc = (qpos[None, :, None, None] - tok).reshape(n_s, t_s, k_sel * BLK)
    new_blk = jnp.pad(nsa_s.reshape(n_s, t_s * NSA_ROWS, DH_A), ((0, 0), (0, BLK_ROWS - t_s * NSA_ROWS), (0, 0)))
    wnew = jnp.pad(win_s.reshape(n_s, t_s * 2, DH_A), ((0, 0), (0, 16 - t_s * 2), (0, 0)))
    gs = gates_s[:, :3 * H_A].reshape(n_s, rows_s, 3)
    g0, g1, g2 = (jnp.broadcast_to(gs[:, :, i:i + 1], (n_s, rows_s, DH_A)) for i in range(3))
    o_nsa_s = _slc_sample(page_table, sel.reshape(n_s, t_s * k_sel), cache2d, q32, rel_slc, new_blk,
                          state_win_kv.reshape(n_s, wb * 2, DH_A), wnew, tbl_t, ocmp_s, g0, g1, g2, nbk, k_sel)

    wuk_t = w_uk.transpose(1, 2, 0).astype(BF16)
    qlat = _headproj(qnope_s, wuk_t).reshape(n_s, rows_s, KV_LORA).astype(BF16)
    qrope_s3 = qrope_s.reshape(n_s, t_s, H_B, LANES)[..., :D_ROPE].reshape(n_s, rows_s, D_ROPE)
    q576 = jnp.concatenate([qlat, qrope_s3], axis=-1)
    mla_new = jnp.pad(mla_s.reshape(n_s, t_s, MLA_ROW), ((0, 0), (0, 8 - t_s), (0, 0)))
    o_lat = _mla_sample(page_table, cache_t, q576, mla_new)
    wuv_h = w_uv.transpose(1, 0, 2).astype(BF16)
    o_mla_s = _headproj(o_lat.reshape(n_s * t_s, H_B * KV_LORA), wuv_h)

    x1_p, h2_p = _outproj(o_nsa_p, o_mla_p, w_out_b, xp2, gt_a, sc_f, sh_f, g_ffn, seq)
    y_p = _ffn(h2_p, wg_b, wu_b, wd_b, x1_p, gt_f, g_final, seq)
    oa_s = o_nsa_s.reshape(n_s * t_s, H_A * DH_A).astype(BF16)
    x1_s, h2_s = _outproj(oa_s, o_mla_s.astype(BF16), w_out_b, xs2, gt_a_s, sc_f_s, sh_f_s, g_ffn, t_s)
    y_s = _ffn(h2_s, wg_b, wu_b, wd_b, x1_s, gt_f_s, g_final, t_s)

    win_all = jnp.concatenate([state_win_kv, win_s.reshape(n_s, t_s, 2, DH_A)], axis=1)
    return (y_p.reshape(n_p, seq, d), y_s.reshape(n_s, t_s, d),
            nsa_p.reshape(n_p, seq, 4, DH_A), mla_p.reshape(n_p, seq, MLA_ROW),
            win_p.reshape(n_p, seq, 2 * DH_A)[:, -min(WINDOW, seq):].reshape(n_p, -1, 2, DH_A),
            nsa_s.reshape(n_s, t_s, 4, DH_A), mla_s.reshape(n_s, t_s, MLA_ROW),
            win_all[:, -min(WINDOW, wb + t_s):])
```

```python
import functools
import math

import numpy as np
import jax
import jax.numpy as jnp
from jax import lax
from jax.experimental import pallas as pl
from jax.experimental.pallas import tpu as pltpu

F32 = jnp.float32
BF16 = jnp.bfloat16
I32 = jnp.int32

H_A = 8
DH_A = 128
BLK = 64
N_SEL = 16
WINDOW = 512
H_B = 8
D_NOPE = 128
D_ROPE = 64
D_V = 128
Q_LORA = 512
KV_LORA = 512
MLA_ROW = KV_LORA + D_ROPE
PAGE_SIZE = 128
ROPE_THETA = 10000.0
N_BUCKETS = 32
T5_MAX_DIST = 128
LOG2E = math.log2(math.e)
QSCALE_A = DH_A ** -0.5 * LOG2E
QSCALE_B = (D_NOPE + D_ROPE) ** -0.5 * LOG2E
EPS = 1e-6
NEG_INF = -1e30
FORCED_SCORE = 1e4

Q_TILE = 128
K_TILE = 512
MLA_K_TILE = 512
NEW_BLOCK_SLOT = 2
LANES = 128
V7X_VMEM_LIMIT = 56 * 1024 * 1024
NSA_ROWS = 4
PAGE_ROWS = PAGE_SIZE * NSA_ROWS
BLK_ROWS = BLK * NSA_ROWS
CHUNK_PAGES = 32
CMP_RING = 3
MLA_RING = 3
DMA_UNROLL = 4


def _t5_bucket_np(n):
    n = np.maximum(np.asarray(n, np.int64), 0)
    exact = N_BUCKETS // 2
    nf = np.maximum(n, 1).astype(np.float32)
    far = exact + (np.log(nf / np.float32(exact)) / np.float32(math.log(T5_MAX_DIST / exact))
                   * np.float32(N_BUCKETS - exact)).astype(np.int32)
    return np.where(n < exact, n, np.minimum(far, N_BUCKETS - 1)).astype(np.int32)


_BUCKET_START = [int(np.argmax(_t5_bucket_np(np.arange(4 * T5_MAX_DIST)) >= b)) for b in range(N_BUCKETS)]


def _dot(a, b):
    return jnp.dot(a, b, preferred_element_type=F32)


def _dot_nt(a, b):
    return lax.dot_general(a, b, (((1,), (1,)), ((), ())), preferred_element_type=F32)


def _rms(x, g):
    return x * lax.rsqrt(jnp.mean(x * x, axis=-1, keepdims=True) + EPS) * g


def _cparams(sem, vmem=V7X_VMEM_LIMIT):
    return pltpu.CompilerParams(dimension_semantics=sem, vmem_limit_bytes=vmem)


def _const_spec(shape):
    nd = len(shape)
    return pl.BlockSpec(shape, lambda *_: (0,) * nd, pipeline_mode=pl.Buffered(1))


def _smem_spec():
    return pl.BlockSpec(memory_space=pltpu.SMEM)


def _bias_from_smem(n, tbl_ref, h, offset=0.0):
    bias = jnp.full(n.shape, 0.0, F32) + (tbl_ref[N_BUCKETS - 1, h] - offset) * LOG2E
    for b in range(N_BUCKETS - 2, -1, -1):
        bias = jnp.where(n < _BUCKET_START[b + 1], (tbl_ref[b, h] - offset) * LOG2E, bias)
    return bias


def _bias_from_cols(n, tbl_ref):
    bias = jnp.broadcast_to(tbl_ref[:, N_BUCKETS - 1:N_BUCKETS], (tbl_ref.shape[0], n.shape[1]))
    for b in range(N_BUCKETS - 2, -1, -1):
        bias = jnp.where(n < _BUCKET_START[b + 1], tbl_ref[:, b:b + 1], bias)
    return bias * LOG2E


def _mod_kernel(c_ref, w_ref, b_ref, o_ref):
    c = c_ref[...]
    s = (c * jax.nn.sigmoid(c)).astype(BF16)
    o_ref[...] = _dot(s, w_ref[...].astype(BF16)) + b_ref[...]


def _modulation(c_all, w_ada, b_ada):
    m, d = c_all.shape
    n = w_ada.shape[1]
    tn = 512
    return pl.pallas_call(
        _mod_kernel,
        grid=(n // tn,),
        in_specs=[pl.BlockSpec((m, d), lambda j: (0, 0)),
                  pl.BlockSpec((d, tn), lambda j: (0, j)),
                  pl.BlockSpec((1, tn), lambda j: (0, j))],
        out_specs=pl.BlockSpec((m, tn), lambda j: (0, j)),
        out_shape=jax.ShapeDtypeStruct((m, n), F32),
        compiler_params=_cparams(("arbitrary",)),
        name="adaln_mod",
    )(c_all, w_ada, b_ada.reshape(1, n))


C_QN = 0
C_KV = C_QN + H_A * DH_A
C_WIN = C_KV + 4 * DH_A
C_CQ = C_WIN + 2 * DH_A
C_CKV = C_CQ + Q_LORA
C_KR = C_CKV + KV_LORA
C_KRR = C_KR + LANES
C_G = C_KRR + LANES
C_END = C_G + LANES


def _inproj_kernel(x_ref, sc_ref, sh_ref, g_ref, w_ref, gq_ref, wuq_ref, gkv_ref, wkv_ref, cos_ref, sin_ref,
                   qn_ref, nsa_ref, win_ref, gates_ref, mla_ref, qnope_ref, qrope_ref, krp_ref, knope_ref, v_ref,
                   kvb_ref):
    x = x_ref[...]
    h = _rms(x, g_ref[...]) * (1.0 + sc_ref[...]) + sh_ref[...]
    z = _dot(h.astype(BF16), w_ref[...])
    qn_ref[...] = (z[:, C_QN:C_KV] * QSCALE_A).astype(BF16)
    nsa_ref[...] = z[:, C_KV:C_WIN]
    win_ref[...] = z[:, C_WIN:C_CQ]
    kvb_ref[...] = z[:, C_KV:C_CQ].astype(BF16)
    gates_ref[...] = jax.nn.sigmoid(z[:, C_G:C_END])
    cos = cos_ref[...]
    sin = sin_ref[...]
    kvn = _rms(z[:, C_CKV:C_KR], gkv_ref[...])
    krp = z[:, C_KR:C_KRR] * cos + z[:, C_KRR:C_G] * sin
    mla_ref[:, 0:KV_LORA] = kvn
    mla_ref[:, KV_LORA:MLA_ROW] = krp[:, 0:D_ROPE]
    krp_ref[...] = krp.astype(BF16)
    cqn = _rms(z[:, C_CQ:C_CKV], gq_ref[...])
    q = _dot(cqn.astype(BF16), wuq_ref[...])
    nq = H_B * D_NOPE
    qnope_ref[...] = (q[:, 0:nq] * QSCALE_B).astype(BF16)
    for hh in range(H_B):
        a = q[:, nq + hh * LANES: nq + (hh + 1) * LANES]
        b = q[:, 2 * nq + hh * LANES: 2 * nq + (hh + 1) * LANES]
        qrope_ref[:, hh * LANES:(hh + 1) * LANES] = ((a * cos + b * sin) * QSCALE_B).astype(BF16)
    kv = _dot(kvn.astype(BF16), wkv_ref[...])
    knope_ref[...] = kv[:, 0:nq].astype(BF16)
    v_ref[...] = kv[:, nq:2 * nq].astype(BF16)


def _mod_spec(arr, tm, rows_per_group):
    d = arr.shape[-1]
    if arr.shape[1] != 1:
        return pl.BlockSpec((None, tm, d), lambda i, *_: (0, i, 0))
    tps = rows_per_group // tm
    return pl.BlockSpec((None, 1, d), lambda i, *_: (i // tps, 0, 0))


def _inproj(x2d, sc, sh, g_attn, w_ext, g_q, wuq_ext, g_kv, wkv, cos_t, sin_t, rows_per_group):
    rows, d = x2d.shape
    tm = min(512, rows)
    nt = rows // tm
    mod_spec = _mod_spec(sc, tm, rows_per_group)
    ptiles = cos_t.shape[0] // tm
    pos_spec = pl.BlockSpec((tm, LANES), lambda i: (i % ptiles, 0))
    row = lambda n: pl.BlockSpec((tm, n), lambda i: (i, 0))
    nq = H_B * D_NOPE
    outs = [(H_A * DH_A, BF16), (4 * DH_A, F32), (2 * DH_A, F32), (LANES, F32), (MLA_ROW, F32),
            (nq, BF16), (H_B * LANES, BF16), (LANES, BF16), (nq, BF16), (nq, BF16), (6 * DH_A, BF16)]
    return pl.pallas_call(
        _inproj_kernel,
        grid=(nt,),
        in_specs=[row(d), mod_spec, mod_spec, _const_spec((1, d)), _const_spec(w_ext.shape),
                  _const_spec((1, Q_LORA)), _const_spec(wuq_ext.shape), _const_spec((1, KV_LORA)),
                  _const_spec(wkv.shape), pos_spec, pos_spec],
        out_specs=[row(n) for n, _ in outs],
        out_shape=[jax.ShapeDtypeStruct((rows, n), dt) for n, dt in outs],
        compiler_params=_cparams(("arbitrary",)),
        name="inproj",
    )(x2d, sc, sh, g_attn.reshape(1, d), w_ext, g_q.reshape(1, Q_LORA), wuq_ext, g_kv.reshape(1, KV_LORA),
      wkv, cos_t, sin_t)


def _compress_kernel(kb_ref, vb_ref, w1k_ref, w2k_ref, w1v_ref, w2v_ref, ck_ref, cv_ref):
    hk = jax.nn.gelu(_dot(kb_ref[...].astype(BF16), w1k_ref[...]))
    ck_ref[...] = _dot(hk.astype(BF16), w2k_ref[...])
    hv = jax.nn.gelu(_dot(vb_ref[...].astype(BF16), w1v_ref[...]))
    cv_ref[...] = _dot(hv.astype(BF16), w2v_ref[...])


def _compress_prompt(kb, vb, w1k, w2k, w1v, w2v):
    nblk = kb.shape[0]
    full = lambda a: pl.BlockSpec(a.shape, lambda i: (0,) * a.ndim)
    args = (kb, vb, w1k, w2k, w1v, w2v)
    return pl.pallas_call(
        _compress_kernel,
        grid=(1,),
        in_specs=[full(a) for a in args],
        out_specs=[pl.BlockSpec((nblk, DH_A), lambda i: (0, 0))] * 2,
        out_shape=[jax.ShapeDtypeStruct((nblk, DH_A), F32)] * 2,
        compiler_params=_cparams(("arbitrary",)),
        name="compress_prompt",
    )(*args)


def _flash_reset(m_ref, l_ref, a_ref):
    m_ref[...] = jnp.full(m_ref.shape, NEG_INF, F32)
    l_ref[...] = jnp.zeros(l_ref.shape, F32)
    a_ref[...] = jnp.zeros(a_ref.shape, F32)


def _flash_tile(m_ref, l_ref, a_ref, logits, v_of):
    probs = []
    for h, s in enumerate(logits):
        m = m_ref[h]
        m_new = jnp.maximum(m, jnp.max(s, axis=0, keepdims=True))
        alpha = jnp.exp2(m - m_new)
        p = jnp.exp2(s - m_new)
        l_ref[h] = alpha * l_ref[h] + jnp.sum(p, axis=0, keepdims=True)
        m_ref[h] = m_new
        probs.append((alpha, p.astype(BF16)))
    for h, (alpha, p) in enumerate(probs):
        a_ref[h] = alpha * a_ref[h] + _dot(v_of(h), p)


def _nsa_prompt_kernel(tbl_ref, q_ref, gt_ref, ck_ref, cvt_ref, ks_ref, vst_ref, kw_ref, vwt_ref, o_ref,
                       out_ref, m_ref, l_ref, a_ref, band_ref, cband_ref, score_ref, self_ref, selr_ref, madd_ref,
                       nearm_ref, *, nb, k_sel):
    qb = pl.program_id(1)
    qs = qb * Q_TILE
    t_iota = lax.broadcasted_iota(I32, (1, Q_TILE), 1)
    tpos = qs + t_iota
    b_iota = lax.broadcasted_iota(I32, (nb, 1), 0)
    kp_iota = lax.broadcasted_iota(I32, (2 * Q_TILE, 1), 0)

    @pl.when((pl.program_id(0) == 0) & (qb == 0))
    def _():
        n_band = jnp.maximum(t_iota - kp_iota + Q_TILE, 0)
        j_iota = lax.broadcasted_iota(I32, (cband_ref.shape[1], 1), 0)
        n_cb = jnp.maximum(t_iota - (BLK - 1) - BLK * (j_iota - 2), 0)
        for h in range(H_A):
            band_ref[h] = _bias_from_smem(n_band, tbl_ref, h, offset=tbl_ref[N_BUCKETS - 1, h])
            cband_ref[h] = _bias_from_smem(n_cb, tbl_ref, h)

    blk_end = b_iota * BLK + (BLK - 1)
    valid_c = blk_end <= tpos
    valid_cf = valid_c.astype(F32)
    j_rel = b_iota - 2 * qb + 2
    ck = ck_ref[...]
    cvt = cvt_ref[...]

    def cmp_bias(h):
        bias = jnp.full((nb, Q_TILE), 0.0, F32) + tbl_ref[N_BUCKETS - 1, h] * LOG2E
        for j in range(4):
            bias = jnp.where(j_rel == j, cband_ref[h, j:j + 1, :], bias)
        return bias

    imp = jnp.zeros((nb, Q_TILE), F32)
    lgs = [_dot_nt(ck, q_ref[:, h * DH_A:(h + 1) * DH_A]) for h in range(H_A)]
    ps = []
    for h in range(H_A):
        lg = jnp.where(valid_c, lgs[h] + cmp_bias(h), NEG_INF)
        m = jnp.max(lg, axis=0, keepdims=True)
        p = jnp.exp2(lg - m) * valid_cf
        p = p / jnp.maximum(jnp.sum(p, axis=0, keepdims=True), 1e-30)
        imp = imp + p
        ps.append(p.astype(BF16))
    for h in range(H_A):
        out_ref[h] = gt_ref[3 * h:3 * h + 1, :] * _dot(cvt, ps[h])

    cur = tpos // BLK
    forced = (b_iota == 0) | (b_iota == cur) | (b_iota == cur - 1)
    score = jnp.where(forced, FORCED_SCORE, jnp.where(b_iota <= cur, imp, -FORCED_SCORE))
    score_ref[...] = score

    def rank_body(bp, rank):
        row = score_ref[pl.ds(bp, 1), :]
        tie = jnp.where(bp < b_iota, 1, 0)
        return rank + jnp.where(row > score, 1, jnp.where(row == score, tie, 0))

    rank = lax.fori_loop(0, nb, rank_body, jnp.zeros((nb, Q_TILE), I32), unroll=4)
    sel = rank < k_sel
    selr_ref[...] = sel.astype(F32)
    self_ref[...] = jnp.where(sel & (b_iota < 2 * qb - 2), 0.0, NEG_INF)
    nfar = (jnp.maximum(qb - 1, 0) * Q_TILE + (K_TILE - 1)) // K_TILE
    blocks_per_tile = K_TILE // BLK

    def mask_body(j, c):
        for i in range(blocks_per_tile):
            row = self_ref[pl.ds(j * blocks_per_tile + i, 1), :]
            madd_ref[pl.ds(pl.multiple_of(j * K_TILE + i * BLK, BLK), BLK), :] = jnp.broadcast_to(row, (BLK, Q_TILE))
        return c

    lax.fori_loop(0, nfar, mask_body, 0)

    causal = (kp_iota - Q_TILE) <= t_iota
    in_seq = (kp_iota + qs) >= Q_TILE
    for i in range(4):
        bi = 2 * qb - 2 + i
        row = selr_ref[pl.ds(jnp.maximum(bi, 0), 1), :]
        ok = (row > 0.5) & causal[i * BLK:(i + 1) * BLK] & (bi >= 0)
        nearm_ref[i * BLK:(i + 1) * BLK, :] = jnp.where(ok, 0.0, NEG_INF)
    win_near = jnp.where(causal & in_seq, 0.0, NEG_INF)
    nwf = WINDOW - Q_TILE
    wf_iota = lax.broadcasted_iota(I32, (nwf, 1), 0)
    win_far = jnp.where((wf_iota >= t_iota) & ((wf_iota + qs) >= WINDOW), 0.0, NEG_INF)

    _flash_reset(m_ref, l_ref, a_ref)
    heads = range(H_A)
    q_of = lambda h: q_ref[:, h * DH_A:(h + 1) * DH_A]

    def far_body(j, c):
        r0 = pl.multiple_of(Q_TILE + j * K_TILE, Q_TILE)
        kt = ks_ref[pl.ds(r0, K_TILE), :]
        vt = vst_ref[:, pl.ds(r0, K_TILE)]
        mt = madd_ref[pl.ds(pl.multiple_of(j * K_TILE, K_TILE), K_TILE), :]
        _flash_tile(m_ref, l_ref, a_ref, [_dot_nt(kt, q_of(h)) + mt for h in heads], lambda h: vt)
        return c

    lax.fori_loop(0, nfar, far_body, 0)
    near0 = pl.multiple_of(qs, Q_TILE)
    kt = ks_ref[pl.ds(near0, 2 * Q_TILE), :]
    vt = vst_ref[:, pl.ds(near0, 2 * Q_TILE)]
    nearm = nearm_ref[...]
    _flash_tile(m_ref, l_ref, a_ref, [_dot_nt(kt, q_of(h)) + (band_ref[h] + nearm) for h in heads], lambda h: vt)
    for h in heads:
        out_ref[h] += gt_ref[3 * h + 1:3 * h + 2, :] * (a_ref[h] / jnp.maximum(l_ref[h], 1e-30))

    _flash_reset(m_ref, l_ref, a_ref)
    kt = kw_ref[pl.ds(near0, nwf), :]
    vt = vwt_ref[:, pl.ds(near0, nwf)]
    _flash_tile(m_ref, l_ref, a_ref, [_dot_nt(kt, q_of(h)) + win_far for h in heads], lambda h: vt)
    wn0 = pl.multiple_of(qs + nwf, Q_TILE)
    kt2 = kw_ref[pl.ds(wn0, 2 * Q_TILE), :]
    vt2 = vwt_ref[:, pl.ds(wn0, 2 * Q_TILE)]
    _flash_tile(m_ref, l_ref, a_ref, [_dot_nt(kt2, q_of(h)) + (band_ref[h] + win_near) for h in heads], lambda h: vt2)
    for h in heads:
        tot = out_ref[h] + gt_ref[3 * h + 2:3 * h + 3, :] * (a_ref[h] / jnp.maximum(l_ref[h], 1e-30))
        o_ref[:, h * DH_A:(h + 1) * DH_A] = tot.T.astype(BF16)


def _nsa_prompt(rel_table, qn, gates_t, ck, cvt, ks_p, vst_p, kw_p, vwt_p, n_seq, seq):
    nb = seq // BLK
    nq = seq // Q_TILE
    k_sel = min(N_SEL, nb)
    per_seq = lambda a: pl.BlockSpec((None,) + a.shape[1:], lambda n, i: (n,) + (0,) * (a.ndim - 1),
                                     pipeline_mode=pl.Buffered(1))
    kern = functools.partial(_nsa_prompt_kernel, nb=nb, k_sel=k_sel)
    stat = pltpu.VMEM((H_A, 1, Q_TILE), F32)
    return pl.pallas_call(
        kern,
        grid=(n_seq, nq),
        in_specs=[_smem_spec(),
                  pl.BlockSpec((Q_TILE, H_A * DH_A), lambda n, i: (n * nq + i, 0)),
                  pl.BlockSpec((None, 3 * H_A, Q_TILE), lambda n, i: (n, 0, i)),
                  per_seq(ck), per_seq(cvt), per_seq(ks_p), per_seq(vst_p), per_seq(kw_p), per_seq(vwt_p)],
        out_specs=pl.BlockSpec((Q_TILE, H_A * DH_A), lambda n, i: (n * nq + i, 0)),
        out_shape=jax.ShapeDtypeStruct((n_seq * seq, H_A * DH_A), BF16),
        scratch_shapes=[pltpu.VMEM((H_A, DH_A, Q_TILE), F32), stat, stat, pltpu.VMEM((H_A, DH_A, Q_TILE), F32),
                        pltpu.VMEM((H_A, 2 * Q_TILE, Q_TILE), F32), pltpu.VMEM((H_A, 8, Q_TILE), F32),
                        pltpu.VMEM((nb, Q_TILE), F32), pltpu.VMEM((nb, Q_TILE), F32), pltpu.VMEM((nb, Q_TILE), F32),
                        pltpu.VMEM((seq, Q_TILE), F32), pltpu.VMEM((2 * Q_TILE, Q_TILE), F32)],
        compiler_params=_cparams(("arbitrary", "arbitrary")),
        name="nsa_prompt",
    )(rel_table, qn, gates_t, ck, cvt, ks_p, vst_p, kw_p, vwt_p)


def _mla_prompt_kernel(qn_ref, qr_ref, kn_ref, kr_ref, vt_ref, o_ref, m_ref, l_ref, a_ref, *, tk):
    qb = pl.program_id(1)
    qs = qb * Q_TILE
    nfull = qs // tk
    t_iota = lax.broadcasted_iota(I32, (1, Q_TILE), 1)
    k_iota = lax.broadcasted_iota(I32, (tk, 1), 0)
    diag0 = pl.multiple_of(nfull * tk, tk)
    diag_mask = jnp.where((k_iota + diag0) <= (t_iota + qs), 0.0, NEG_INF)
    _flash_reset(m_ref, l_ref, a_ref)

    def tile(r0, mask):
        kr = kr_ref[pl.ds(r0, tk), :]
        logits = []
        for h in range(H_B):
            kh = jnp.concatenate([kn_ref[pl.ds(r0, tk), h * D_NOPE:(h + 1) * D_NOPE], kr], axis=1)
            qh = jnp.concatenate([qn_ref[:, h * D_NOPE:(h + 1) * D_NOPE], qr_ref[:, h * LANES:(h + 1) * LANES]], axis=1)
            s = _dot_nt(kh, qh)
            logits.append(s if mask is None else s + mask)
        _flash_tile(m_ref, l_ref, a_ref, logits, lambda h: vt_ref[h * D_V:(h + 1) * D_V, pl.ds(r0, tk)])

    def body(j, c):
        tile(pl.multiple_of(j * tk, tk), None)
        return c

    lax.fori_loop(0, nfull, body, 0)
    tile(diag0, diag_mask)
    for h in range(H_B):
        o_ref[:, h * D_V:(h + 1) * D_V] = (a_ref[h] / jnp.maximum(l_ref[h], 1e-30)).T.astype(BF16)


def _mla_prompt(qnope, qrope, knope, krp, vt, n_seq, seq):
    nq = seq // Q_TILE
    tk = min(MLA_K_TILE, seq)
    per_seq = lambda a: pl.BlockSpec((None,) + a.shape[1:], lambda n, i: (n,) + (0,) * (a.ndim - 1),
                                     pipeline_mode=pl.Buffered(1))
    qspec = lambda w: pl.BlockSpec((Q_TILE, w), lambda n, i: (n * nq + i, 0))
    stat = pltpu.VMEM((H_B, 1, Q_TILE), F32)
    return pl.pallas_call(
        functools.partial(_mla_prompt_kernel, tk=tk),
        grid=(n_seq, nq),
        in_specs=[qspec(H_B * D_NOPE), qspec(H_B * LANES), per_seq(knope), per_seq(krp), per_seq(vt)],
        out_specs=qspec(H_B * D_V),
        out_shape=jax.ShapeDtypeStruct((n_seq * seq, H_B * D_V), BF16),
        scratch_shapes=[stat, stat, pltpu.VMEM((H_B, D_V, Q_TILE), F32)],
        compiler_params=_cparams(("arbitrary", "arbitrary")),
        name="mla_prompt",
    )(qnope, qrope, knope, krp, vt)


def _outproj_kernel(oa_ref, ob_ref, w_ref, x_ref, gt_ref, sc_ref, sh_ref, g_ref, x1_ref, h_ref):
    half = oa_ref.shape[1]
    mixed = _dot(oa_ref[...], w_ref[0:half, :]) + _dot(ob_ref[...], w_ref[half:2 * half, :])
    x1 = x_ref[...] + gt_ref[...] * mixed
    x1_ref[...] = x1
    h_ref[...] = (_rms(x1, g_ref[...]) * (1.0 + sc_ref[...]) + sh_ref[...]).astype(BF16)


def _outproj(oa, ob, w_out, x2d, gt, sc, sh, g_ffn, rows_per_group):
    rows, d = x2d.shape
    tm = min(512, rows)
    row = lambda n: pl.BlockSpec((tm, n), lambda i: (i, 0))
    ms = _mod_spec(gt, tm, rows_per_group)
    return pl.pallas_call(
        _outproj_kernel,
        grid=(rows // tm,),
        in_specs=[row(oa.shape[1]), row(ob.shape[1]), _const_spec(w_out.shape), row(d), ms, ms, ms,
                  _const_spec((1, d))],
        out_specs=[row(d), row(d)],
        out_shape=[jax.ShapeDtypeStruct((rows, d), F32), jax.ShapeDtypeStruct((rows, d), BF16)],
        compiler_params=_cparams(("arbitrary",)),
        name="outproj",
    )(oa, ob, w_out, x2d, gt, sc, sh, g_ffn.reshape(1, d))


def _ffn_kernel(h_ref, wg_ref, wu_ref, wd_ref, x1_ref, gt_ref, g_ref, y_ref, acc_ref):
    j = pl.program_id(1)
    h = h_ref[...]
    a = _dot(h, wg_ref[...])
    u = _dot(h, wu_ref[...])
    part = _dot(((a * jax.nn.sigmoid(a)) * u).astype(BF16), wd_ref[...])

    @pl.when(j == 0)
    def _():
        acc_ref[...] = part

    @pl.when(j > 0)
    def _():
        acc_ref[...] += part

    @pl.when(j == pl.num_programs(1) - 1)
    def _():
        x2 = x1_ref[...] + gt_ref[...] * acc_ref[...]
        y_ref[...] = _rms(x2, g_ref[...])


def _ffn(h2, wg, wu, wd, x1, gt, g_final, rows_per_group):
    rows, d = x1.shape
    dff = wg.shape[1]
    tm = min(512, rows)
    tf = 512
    row = lambda n: pl.BlockSpec((tm, n), lambda i, j: (i, 0))
    return pl.pallas_call(
        _ffn_kernel,
        grid=(rows // tm, dff // tf),
        in_specs=[row(d), pl.BlockSpec((d, tf), lambda i, j: (0, j)), pl.BlockSpec((d, tf), lambda i, j: (0, j)),
                  pl.BlockSpec((tf, d), lambda i, j: (j, 0)), row(d), _mod_spec(gt, tm, rows_per_group),
                  pl.BlockSpec((1, d), lambda i, j: (0, 0))],
        out_specs=row(d),
        out_shape=jax.ShapeDtypeStruct((rows, d), F32),
        scratch_shapes=[pltpu.VMEM((tm, d), F32)],
        compiler_params=_cparams(("arbitrary", "arbitrary")),
        name="ffn",
    )(h2, wg, wu, wd, x1, gt, g_final.reshape(1, d))


def _headproj_kernel(x_ref, w_ref, o_ref):
    o_ref[...] = _dot(x_ref[...].astype(BF16), w_ref[...])


def _headproj(x, w):
    rows = x.shape[0]
    nh, k, n = w.shape
    return pl.pallas_call(
        _headproj_kernel,
        grid=(nh,),
        in_specs=[pl.BlockSpec((rows, k), lambda h: (0, h)), pl.BlockSpec((None, k, n), lambda h: (h, 0, 0))],
        out_specs=pl.BlockSpec((rows, n), lambda h: (0, h)),
        out_shape=jax.ShapeDtypeStruct((rows, nh * n), F32),
        compiler_params=_cparams(("arbitrary",)),
        name="headproj",
    )(x, w)


def _chunk_page_copy(cache_ref, pt_ref, buf_ref, sem_ref, step, nchunk, cpages, ring, i):
    seq = step // nchunk
    c = step % nchunk
    slot = step % ring
    page = pt_ref[seq, c * cpages + i]
    return pltpu.make_async_copy(cache_ref.at[pl.ds(pl.multiple_of(page * PAGE_ROWS, PAGE_ROWS), PAGE_ROWS), :],
                                 buf_ref.at[slot, pl.ds(pl.multiple_of(i * PAGE_ROWS, PAGE_ROWS), PAGE_ROWS), :],
                                 sem_ref.at[slot])


def _ring_schedule(start_chunk, s, nstep, ring):
    ahead = ring - 1

    @pl.when(s == 0)
    def _():
        for d in range(ahead):
            @pl.when(d < nstep)
            def _():
                start_chunk(d)

    @pl.when(s + ahead < nstep)
    def _():
        start_chunk(s + ahead)


def _cmp_sample_kernel(pt_ref, cache_ref, q_ref, w1k_ref, w2k_ref, w1v_ref, w2v_ref, tbl_ref,
                       ocmp_ref, score_ref, buf_ref, sem_ref, stage_ref, *, nchunk, cpages, nbk, pitch, past):
    s = pl.program_id(0)
    nstep = pl.num_programs(0)
    c = s % nchunk
    slot = s % CMP_RING

    def start_chunk(step):
        def body(i, cc):
            _chunk_page_copy(cache_ref, pt_ref, buf_ref, sem_ref, step, nchunk, cpages, CMP_RING, i).start()
            return cc
        lax.fori_loop(0, cpages, body, 0, unroll=DMA_UNROLL)

    _ring_schedule(start_chunk, s, nstep, CMP_RING)

    def wait_body(i, cc):
        _chunk_page_copy(cache_ref, pt_ref, buf_ref, sem_ref, s, nchunk, cpages, CMP_RING, i).wait()
        return cc
    lax.fori_loop(0, cpages, wait_body, 0, unroll=DMA_UNROLL)

    for i in range(cpages):
        for half in range(2):
            blk = (c * cpages + i) * 2 + half
            for comp in range(2):
                slab = buf_ref[slot, pl.ds(i * PAGE_ROWS + half * BLK_ROWS + comp, BLK, stride=NSA_ROWS), :]
                stage_ref[comp, pl.ds(blk, BLK, stride=pitch), :] = slab

    @pl.when(c == nchunk - 1)
    def _():
        group = 16

        def compress(comp, w1_ref, w2_ref):
            pre = jnp.zeros((nbk, DH_A), F32)
            for g in range(BLK // group):
                x = jnp.concatenate([stage_ref[comp, pl.ds((g * group + j) * pitch, nbk), :] for j in range(group)],
                                    axis=1).astype(BF16)
                pre = pre + _dot(x, w1_ref[g * group * DH_A:(g + 1) * group * DH_A, :])
            return _dot(jax.nn.gelu(pre).astype(BF16), w2_ref[...]).astype(BF16)

        ck = compress(0, w1k_ref, w2k_ref)
        cv = compress(1, w1v_ref, w2v_ref)
        q = q_ref[...]
        rows = q.shape[0]
        nt = rows // H_A
        t_of_row = lax.broadcasted_iota(I32, (rows, 1), 0) // H_A
        b_iota = lax.broadcasted_iota(I32, (1, nbk), 1)
        n_cmp = jnp.maximum(past + t_of_row - (b_iota * BLK + (BLK - 1)), 0)
        lg = _dot_nt(q, ck) + _bias_from_cols(n_cmp, tbl_ref)
        m = jnp.max(lg, axis=-1, keepdims=True)
        p = jnp.exp2(lg - m)
        p = p / jnp.maximum(jnp.sum(p, axis=-1, keepdims=True), 1e-30)
        ocmp_ref[...] = _dot(p.astype(BF16), cv)

        width = score_ref.shape[-1]
        bid = lax.broadcasted_iota(I32, (1, width), 1)
        imp = jnp.concatenate([jnp.sum(p[t * H_A:(t + 1) * H_A], axis=0, keepdims=True) for t in range(nt)], axis=0)
        imp = jnp.concatenate([imp, jnp.zeros((nt, width - nbk), F32)], axis=1)
        cur = nbk
        forced = (bid == 0) | (bid == cur) | (bid == cur - 1)
        score = jnp.where(forced, FORCED_SCORE, imp)
        score_ref[...] = jnp.where(bid > cur, NEG_INF, score)


def _select_kernel(s_ref, sel_ref, *, nbs, k_sel):
    s = s_ref[...]
    b_iota = lax.broadcasted_iota(I32, (s.shape[0], 1), 0)

    def rank_body(bp, rank):
        row = s_ref[pl.ds(bp, 1), :]
        tie = jnp.where(bp < b_iota, 1, 0)
        return rank + jnp.where(row > s, 1, jnp.where(row == s, tie, 0))

    rank = lax.fori_loop(0, nbs, rank_body, jnp.zeros(s.shape, I32))
    rows = [jnp.sum(jnp.where(rank == r, b_iota, 0), axis=0, keepdims=True) for r in range(k_sel)]
    rows += [jnp.zeros_like(rows[0])] * (sel_ref.shape[0] - k_sel)
    sel_ref[...] = jnp.concatenate(rows, axis=0)


def _select_blocks(score_t, nbs, k_sel):
    width, rows = score_t.shape
    return pl.pallas_call(
        functools.partial(_select_kernel, nbs=nbs, k_sel=k_sel),
        grid=(rows // LANES,),
        in_specs=[pl.BlockSpec((width, LANES), lambda i: (0, i))],
        out_specs=pl.BlockSpec((N_SEL, LANES), lambda i: (0, i)),
        out_shape=jax.ShapeDtypeStruct((N_SEL, rows), I32),
        compiler_params=_cparams(("arbitrary",)),
        name="select_blocks",
    )(score_t)


def _cmp_sample(page_table, cache2d, q32, w1k, w2k, w1v, w2v, tbl_rows, past):
    nseq, npages = page_table.shape
    rows = q32.shape[1]
    nt = rows // H_A
    nbk = past // BLK
    width = -(-(nbk + 1) // LANES) * LANES
    cpages = min(CHUNK_PAGES, npages)
    nchunk = npages // cpages
    pitch = nbk + 8
    kern = functools.partial(_cmp_sample_kernel, nchunk=nchunk, cpages=cpages, nbk=nbk, pitch=pitch, past=past)
    cst = lambda a: pl.BlockSpec(a.shape, lambda s, pt: (0,) * a.ndim, pipeline_mode=pl.Buffered(1))
    per = lambda a: pl.BlockSpec((None,) + a.shape[1:], lambda s, pt: (s // nchunk,) + (0,) * (a.ndim - 1))
    grid_spec = pltpu.PrefetchScalarGridSpec(
        num_scalar_prefetch=1,
        grid=(nseq * nchunk,),
        in_specs=[pl.BlockSpec(memory_space=pl.ANY), per(q32), cst(w1k), cst(w2k), cst(w1v), cst(w2v), cst(tbl_rows)],
        out_specs=[pl.BlockSpec((None, rows, DH_A), lambda s, pt: (s // nchunk, 0, 0)),
                   pl.BlockSpec((None, nt, width), lambda s, pt: (s // nchunk, 0, 0))],
        scratch_shapes=[pltpu.VMEM((CMP_RING, cpages * PAGE_ROWS, DH_A), F32), pltpu.SemaphoreType.DMA((CMP_RING,)),
                        pltpu.VMEM((2, BLK * pitch, DH_A), F32)],
    )
    return pl.pallas_call(
        kern,
        grid_spec=grid_spec,
        out_shape=[jax.ShapeDtypeStruct((nseq, rows, DH_A), F32), jax.ShapeDtypeStruct((nseq, nt, width), F32)],
        compiler_params=_cparams(("arbitrary",)),
        name="cmp_sample",
    )(page_table, cache2d, q32, w1k, w2k, w1v, w2v, tbl_rows)


def _slc_copy(cache_ref, pt_ref, sel_ref, buf_ref, sem_ref, seq, slot, i, nbk):
    blk = jnp.minimum(sel_ref[seq, i], nbk - 1)
    r0 = pl.multiple_of(pt_ref[seq, blk // 2] * PAGE_ROWS + (blk % 2) * BLK_ROWS, BLK_ROWS)
    return pltpu.make_async_copy(cache_ref.at[pl.ds(r0, BLK_ROWS), :], buf_ref.at[slot, i], sem_ref.at[slot])


def _slc_sample_kernel(pt_ref, sel_ref, cache_ref, q_ref, rel_ref, new_ref, state_ref, wnew_ref, tbl_ref,
                       ocmp_ref, g0_ref, g1_ref, g2_ref, o_ref, buf_ref, sem_ref, *, nbk, nslot, nt, wb):
    b = pl.program_id(0)
    nseq = pl.num_programs(0)
    slot = b % 2
    ntot = nt * nslot

    def start_all(seq, sl):
        def body(i, c):
            _slc_copy(cache_ref, pt_ref, sel_ref, buf_ref, sem_ref, seq, sl, i, nbk).start()
            return c
        lax.fori_loop(0, ntot, body, 0, unroll=DMA_UNROLL)

    @pl.when(b == 0)
    def _():
        start_all(0, 0)

    @pl.when(b + 1 < nseq)
    def _():
        start_all(b + 1, 1 - slot)

    def wait_body(i, c):
        _slc_copy(cache_ref, pt_ref, sel_ref, buf_ref, sem_ref, b, slot, i, nbk).wait()
        return c
    lax.fori_loop(0, ntot, wait_body, 0, unroll=DMA_UNROLL)

    for t in range(nt):
        i_new = t * nslot + NEW_BLOCK_SLOT

        @pl.when(sel_ref[b, i_new] >= nbk)
        def _():
            buf_ref[slot, i_new] = new_ref[...]

    q = q_ref[...]
    rows = q.shape[0]
    t_of_row = lax.broadcasted_iota(I32, (rows, 1), 0) // H_A

    kst = state_ref[pl.ds(0, wb, stride=2), :].astype(BF16)
    vst = state_ref[pl.ds(1, wb, stride=2), :].astype(BF16)
    nwn = wnew_ref.shape[0] // 2
    kwn = wnew_ref[pl.ds(0, nwn, stride=2), :].astype(BF16)
    vwn = wnew_ref[pl.ds(1, nwn, stride=2), :].astype(BF16)
    rel_s = wb + t_of_row - lax.broadcasted_iota(I32, (1, wb), 1)
    rel_n = t_of_row - lax.broadcasted_iota(I32, (1, nwn), 1)
    ok_s = rel_s <= WINDOW
    ok_n = rel_n >= 0
    ls = jnp.where(ok_s, _dot_nt(q, kst) + _bias_from_cols(rel_s, tbl_ref), NEG_INF)
    ln = jnp.where(ok_n, _dot_nt(q, kwn) + _bias_from_cols(jnp.maximum(rel_n, 0), tbl_ref), NEG_INF)
    m = jnp.maximum(jnp.max(ls, axis=-1, keepdims=True), jnp.max(ln, axis=-1, keepdims=True))
    ps = jnp.exp2(ls - m) * ok_s.astype(F32)
    pn = jnp.exp2(ln - m) * ok_n.astype(F32)
    den = jnp.maximum(jnp.sum(ps, axis=-1, keepdims=True) + jnp.sum(pn, axis=-1, keepdims=True), 1e-30)
    o_win = _dot((ps / den).astype(BF16), vst) + _dot((pn / den).astype(BF16), vwn)

    blocks = buf_ref.at[slot]
    gather = lambda t, comp: jnp.concatenate(
        [blocks[t * nslot + r, pl.ds(comp, BLK, stride=NSA_ROWS), :] for r in range(nslot)], axis=0).astype(BF16)
    lgs = [_dot_nt(q[t * H_A:(t + 1) * H_A], gather(t, 2)) for t in range(nt)]
    probs = []
    for t in range(nt):
        rel = rel_ref[t:t + 1, :]
        ok = rel >= 0
        lg = jnp.where(ok, lgs[t] + _bias_from_cols(jnp.maximum(rel, 0), tbl_ref.at[0:H_A]), NEG_INF)
        m = jnp.max(lg, axis=-1, keepdims=True)
        p = jnp.exp2(lg - m) * ok.astype(F32)
        probs.append((p / jnp.maximum(jnp.sum(p, axis=-1, keepdims=True), 1e-30)).astype(BF16))
    o_slc = jnp.concatenate([_dot(probs[t], gather(t, 3)) for t in range(nt)], axis=0)
    o_ref[...] = g0_ref[...] * ocmp_ref[...] + g1_ref[...] * o_slc + g2_ref[...] * o_win


def _slc_sample(page_table, sel_flat, cache2d, q32, rel, new_blk, state, wnew, tbl_rows, ocmp, g0, g1, g2, nbk,
                nslot):
    nseq = page_table.shape[0]
    rows = q32.shape[1]
    nt = rows // H_A
    wb = state.shape[1] // 2
    per = lambda a: pl.BlockSpec((None,) + a.shape[1:], lambda b, *_: (b,) + (0,) * (a.ndim - 1))
    kern = functools.partial(_slc_sample_kernel, nbk=nbk, nslot=nslot, nt=nt, wb=wb)
    grid_spec = pltpu.PrefetchScalarGridSpec(
        num_scalar_prefetch=2,
        grid=(nseq,),
        in_specs=[pl.BlockSpec(memory_space=pl.ANY), per(q32), per(rel), per(new_blk), per(state), per(wnew),
                  pl.BlockSpec(tbl_rows.shape, lambda b, *_: (0, 0)), per(ocmp), per(g0), per(g1), per(g2)],
        out_specs=pl.BlockSpec((None, rows, DH_A), lambda b, *_: (b, 0, 0)),
        scratch_shapes=[pltpu.VMEM((2, nt * nslot, BLK_ROWS, DH_A), F32), pltpu.SemaphoreType.DMA((2,))],
    )
    return pl.pallas_call(
        kern,
        grid_spec=grid_spec,
        out_shape=jax.ShapeDtypeStruct((nseq, rows, DH_A), F32),
        compiler_params=_cparams(("arbitrary",)),
        name="slc_sample",
    )(page_table, sel_flat, cache2d, q32, rel, new_blk, state, wnew, tbl_rows, ocmp, g0, g1, g2)


def _mla_page_copy(cache_ref, pt_ref, buf_ref, sem_ref, step, nchunk, cpages, i):
    seq = step // nchunk
    c = step % nchunk
    slot = step % MLA_RING
    page = pt_ref[seq, c * cpages + i]
    return pltpu.make_async_copy(cache_ref.at[page],
                                 buf_ref.at[slot, :, pl.ds(pl.multiple_of(i * PAGE_SIZE, PAGE_SIZE), PAGE_SIZE)],
                                 sem_ref.at[slot])


def _mla_sample_kernel(pt_ref, cache_ref, q_ref, new_ref, o_ref, buf_ref, sem_ref, m_ref, l_ref, acc_ref,
                       *, nchunk, cpages):
    s = pl.program_id(0)
    nstep = pl.num_programs(0)
    c = s % nchunk
    slot = s % MLA_RING

    def start_chunk(step):
        def body(i, cc):
            _mla_page_copy(cache_ref, pt_ref, buf_ref, sem_ref, step, nchunk, cpages, i).start()
            return cc
        lax.fori_loop(0, cpages, body, 0, unroll=DMA_UNROLL)

    _ring_schedule(start_chunk, s, nstep, MLA_RING)
    q = q_ref[...]

    @pl.when(c == 0)
    def _():
        new = new_ref[...]
        sn = _dot_nt(q, new.astype(BF16))
        rows = sn.shape[0]
        t_of_row = lax.broadcasted_iota(I32, (rows, 1), 0) // H_B
        j = lax.broadcasted_iota(I32, (1, new.shape[0]), 1)
        ok = j <= t_of_row
        sn = jnp.where(ok, sn, NEG_INF)
        m = jnp.max(sn, axis=-1, keepdims=True)
        p = jnp.exp2(sn - m) * ok.astype(F32)
        m_ref[...] = m
        l_ref[...] = jnp.sum(p, axis=-1, keepdims=True)
        acc_ref[...] = _dot(p.astype(BF16), new[:, 0:KV_LORA].astype(BF16))

    def wait_body(i, cc):
        _mla_page_copy(cache_ref, pt_ref, buf_ref, sem_ref, s, nchunk, cpages, i).wait()
        return cc
    lax.fori_loop(0, cpages, wait_body, 0, unroll=DMA_UNROLL)

    kv = buf_ref[slot].astype(BF16)
    sc = _dot(q, kv)
    m_old = m_ref[...]
    m_new = jnp.maximum(m_old, jnp.max(sc, axis=-1, keepdims=True))
    alpha = jnp.exp2(m_old - m_new)
    p = jnp.exp2(sc - m_new)
    l_new = alpha * l_ref[...] + jnp.sum(p, axis=-1, keepdims=True)
    acc = alpha * acc_ref[...] + _dot_nt(p.astype(BF16), kv[0:KV_LORA, :])
    m_ref[...] = m_new
    l_ref[...] = l_new
    acc_ref[...] = acc

    @pl.when(c == nchunk - 1)
    def _():
        o_ref[...] = acc / jnp.maximum(l_new, 1e-30)


def _mla_sample(page_table, cache_t, q576, new_rows):
    nseq, npages = page_table.shape
    rows = q576.shape[1]
    cpages = min(CHUNK_PAGES, npages)
    nchunk = npages // cpages
    kern = functools.partial(_mla_sample_kernel, nchunk=nchunk, cpages=cpages)
    per = lambda a: pl.BlockSpec((None,) + a.shape[1:], lambda s, pt: (s // nchunk,) + (0,) * (a.ndim - 1))
    grid_spec = pltpu.PrefetchScalarGridSpec(
        num_scalar_prefetch=1,
        grid=(nseq * nchunk,),
        in_specs=[pl.BlockSpec(memory_space=pl.ANY), per(q576), per(new_rows)],
        out_specs=pl.BlockSpec((None, rows, KV_LORA), lambda s, pt: (s // nchunk, 0, 0)),
        scratch_shapes=[pltpu.VMEM((MLA_RING, MLA_ROW, cpages * PAGE_SIZE), F32), pltpu.SemaphoreType.DMA((MLA_RING,)),
                        pltpu.VMEM((rows, 1), F32), pltpu.VMEM((rows, 1), F32), pltpu.VMEM((rows, KV_LORA), F32)],
    )
    return pl.pallas_call(
        kern,
        grid_spec=grid_spec,
        out_shape=jax.ShapeDtypeStruct((nseq, rows, KV_LORA), F32),
        compiler_params=_cparams(("arbitrary",)),
        name="mla_sample",
    )(page_table, cache_t, q576, new_rows)


def _prep_w_in(w_in):
    d = w_in.shape[0]
    sizes = (H_A * DH_A, 4 * DH_A, 2 * DH_A, 3 * H_A, Q_LORA, KV_LORA, D_ROPE)
    offs = np.concatenate([[0], np.cumsum(sizes)])
    seg = lambda i: w_in[:, int(offs[i]):int(offs[i + 1])]
    zq, zkv, zwin, zg, zcq, zckv, zkr = (seg(i) for i in range(7))
    half = D_ROPE // 2
    zrot = jnp.concatenate([-zkr[:, half:], zkr[:, :half]], axis=1)
    pad = lambda a: jnp.pad(a, ((0, 0), (0, LANES - a.shape[1])))
    w_ext = jnp.concatenate([zq, zkv, zwin, zcq, zckv, pad(zkr), pad(zrot), pad(zg)], axis=1)
    assert w_ext.shape == (d, C_END)
    return w_ext.astype(BF16)


def _prep_w_uq(w_uq):
    r = w_uq.shape[0]
    w = w_uq.reshape(r, H_B, D_NOPE + D_ROPE)
    nope = w[:, :, :D_NOPE].reshape(r, H_B * D_NOPE)
    rope = w[:, :, D_NOPE:]
    half = D_ROPE // 2
    rot = jnp.concatenate([-rope[:, :, half:], rope[:, :, :half]], axis=2)
    padh = lambda a: jnp.pad(a, ((0, 0), (0, 0), (0, LANES - D_ROPE))).reshape(r, H_B * LANES)
    return jnp.concatenate([nope, padh(rope), padh(rot)], axis=1).astype(BF16)


def _rope_tables(pos):
    half = D_ROPE // 2
    inv = ROPE_THETA ** (-jnp.arange(half, dtype=F32) / half)
    ang = pos.astype(F32)[:, None] * inv[None, :]
    cos = jnp.cos(ang)
    sin = jnp.sin(ang)
    z = jnp.zeros((pos.shape[0], LANES - D_ROPE), F32)
    return jnp.concatenate([cos, cos, z], axis=1), jnp.concatenate([sin, sin, z], axis=1)


def kernel(x_prompt, x_sample, cache_nsa_kv, cache_mla, state_win_kv, page_table, c_prompt, c_sample, rel_table,
           g_attn, g_ffn, g_final, w_ada, b_ada, w_in, g_qnorm, w_uq, g_kvnorm, w_uk, w_uv, w_ck1, w_ck2, w_cv1,
           w_cv2, w_out, w_gate, w_up, w_down):
    n_p, seq, d = x_prompt.shape
    n_s, t_s, _ = x_sample.shape
    npages = page_table.shape[1]
    past = npages * PAGE_SIZE
    wb = state_win_kv.shape[1]
    n_pool = cache_nsa_kv.shape[0]
    assert seq % K_TILE == 0 and seq % MLA_K_TILE == 0 and seq >= WINDOW and wb == WINDOW and t_s <= 8
    assert cache_nsa_kv.shape[1:] == (PAGE_SIZE, NSA_ROWS, DH_A) and cache_mla.shape[1:] == (PAGE_SIZE, MLA_ROW)

    w_ext = _prep_w_in(w_in)
    wuq_ext = _prep_w_uq(w_uq)
    wkv = jnp.concatenate([w_uk.reshape(KV_LORA, H_B * D_NOPE), w_uv.reshape(KV_LORA, H_B * D_V)], axis=1).astype(BF16)
    w_out_b = w_out.astype(BF16)
    wg_b, wu_b, wd_b = w_gate.astype(BF16), w_up.astype(BF16), w_down.astype(BF16)
    w1k_b, w2k_b, w1v_b, w2v_b = (w.astype(BF16) for w in (w_ck1, w_ck2, w_cv1, w_cv2))

    n_c = n_p + n_s
    c_all = jnp.pad(jnp.concatenate([c_prompt, c_sample], axis=0), ((0, (-n_c) % 8), (0, 0)))
    mod = _modulation(c_all, w_ada, b_ada)
    mod_p = mod[:n_p].reshape(n_p, 6, 1, d)
    mod_s = jnp.repeat(mod[n_p:n_c].reshape(n_s, 6, d), t_s, axis=0).transpose(1, 0, 2)[:, None]
    sh_a, sc_a, gt_a, sh_f, sc_f, gt_f = (mod_p[:, i] for i in range(6))
    sh_a_s, sc_a_s, gt_a_s, sh_f_s, sc_f_s, gt_f_s = (mod_s[i] for i in range(6))

    cos_p, sin_p = _rope_tables(jnp.arange(seq, dtype=I32))
    pos_s = past + jnp.tile(jnp.arange(t_s, dtype=I32), n_s)
    cos_s, sin_s = _rope_tables(pos_s)
    xp2 = x_prompt.reshape(n_p * seq, d)
    xs2 = x_sample.reshape(n_s * t_s, d)
    (qn_p, nsa_p, win_p, gates_p, mla_p, qnope_p, qrope_p, krp_p, knope_p, v_p, kvb_p) = _inproj(
        xp2, sc_a, sh_a, g_attn, w_ext, g_qnorm, wuq_ext, g_kvnorm, wkv, cos_p, sin_p, seq)
    (qn_s, nsa_s, win_s, gates_s, mla_s, qnope_s, qrope_s, _krp_s, _kn_s, _v_s, _kvb_s) = _inproj(
        xs2, sc_a_s, sh_a_s, g_attn, w_ext, g_qnorm, wuq_ext, g_kvnorm, wkv, cos_s, sin_s, t_s)

    nb = seq // BLK
    kvb_p3 = kvb_p.reshape(n_p, seq, 6 * DH_A)
    part = lambda i: kvb_p3[:, :, i * DH_A:(i + 1) * DH_A]
    kb = part(0).reshape(n_p * nb, BLK * DH_A)
    vb = part(1).reshape(n_p * nb, BLK * DH_A)
    ck_p, cv_p = _compress_prompt(kb, vb, w1k_b, w2k_b, w1v_b, w2v_b)
    ck_p = ck_p.reshape(n_p, nb, DH_A).astype(BF16)
    cvt_p = cv_p.reshape(n_p, nb, DH_A).transpose(0, 2, 1).astype(BF16)
    gates_t = gates_p[:, :3 * H_A].reshape(n_p, seq, 3 * H_A).transpose(0, 2, 1)

    ks, vs, kw, vw = part(2), part(3), part(4), part(5)
    ks_p = jnp.pad(ks, ((0, 0), (Q_TILE, 0), (0, 0)))
    vst_p = jnp.pad(vs, ((0, 0), (Q_TILE, 0), (0, 0))).transpose(0, 2, 1)
    kw_p = jnp.pad(kw, ((0, 0), (WINDOW, 0), (0, 0)))
    vwt_p = jnp.pad(vw, ((0, 0), (WINDOW, 0), (0, 0))).transpose(0, 2, 1)
    o_nsa_p = _nsa_prompt(rel_table, qn_p, gates_t, ck_p, cvt_p, ks_p, vst_p, kw_p, vwt_p, n_p, seq)

    o_mla_p = _mla_prompt(qnope_p, qrope_p, knope_p.reshape(n_p, seq, H_B * D_NOPE), krp_p.reshape(n_p, seq, LANES),
                          v_p.reshape(n_p, seq, H_B * D_V).transpose(0, 2, 1), n_p, seq)

    rows_s = t_s * H_A
    nbk = past // BLK
    nbs = -(-(past + t_s) // BLK)
    assert nbs == nbk + 1 and nbk >= 2
    k_sel = min(N_SEL, nbs)
    assert k_sel > NEW_BLOCK_SLOT and FORCED_SCORE > H_A
    cache2d = cache_nsa_kv.reshape(n_pool * PAGE_ROWS, DH_A)
    cache_t = cache_mla.transpose(0, 2, 1)
    q32 = qn_s.reshape(n_s, rows_s, DH_A)
    tbl_rows = jnp.tile(rel_table.T, (t_s, 1))

    ocmp_s, score = _cmp_sample(page_table, cache2d, q32, w1k_b, w2k_b, w1v_b, w2v_b, tbl_rows, past)
    n_q = n_s * t_s
    score_t = jnp.pad(score.reshape(n_q, -1), ((0, (-n_q) % LANES), (0, 0))).T
    sel = _select_blocks(score_t, nbs, k_sel).T[:n_q, :k_sel].reshape(n_s, t_s, k_sel)

    qpos = past + jnp.arange(t_s, dtype=I32)
    tok = sel[..., None] * BLK + jnp.arange(BLK, dtype=I32)
    rel_slc = (qpos[None, :, None, None] - tok).reshape(n_s, t_s, k_sel * BLK)
    new_blk = jnp.pad(nsa_s.reshape(n_s, t_s * NSA_ROWS, DH_A), ((0, 0), (0, BLK_ROWS - t_s * NSA_ROWS), (0, 0)))
    wnew = jnp.pad(win_s.reshape(n_s, t_s * 2, DH_A), ((0, 0), (0, 16 - t_s * 2), (0, 0)))
    gs = gates_s[:, :3 * H_A].reshape(n_s, rows_s, 3)
    g0, g1, g2 = (jnp.broadcast_to(gs[:, :, i:i + 1], (n_s, rows_s, DH_A)) for i in range(3))
    o_nsa_s = _slc_sample(page_table, sel.reshape(n_s, t_s * k_sel), cache2d, q32, rel_slc, new_blk,
                          state_win_kv.reshape(n_s, wb * 2, DH_A), wnew, tbl_rows, ocmp_s, g0, g1, g2, nbk, k_sel)

    wuk_t = w_uk.transpose(1, 2, 0).astype(BF16)
    qlat = _headproj(qnope_s, wuk_t).reshape(n_s, rows_s, KV_LORA).astype(BF16)
    qrope_s3 = qrope_s.reshape(n_s, t_s, H_B, LANES)[..., :D_ROPE].reshape(n_s, rows_s, D_ROPE)
    q576 = jnp.concatenate([qlat, qrope_s3], axis=-1)
    mla_new = jnp.pad(mla_s.reshape(n_s, t_s, MLA_ROW), ((0, 0), (0, 8 - t_s), (0, 0)))
    o_lat = _mla_sample(page_table, cache_t, q576, mla_new)
    wuv_h = w_uv.transpose(1, 0, 2).astype(BF16)
    o_mla_s = _headproj(o_lat.reshape(n_s * t_s, H_B * KV_LORA), wuv_h)

    x1_p, h2_p = _outproj(o_nsa_p, o_mla_p, w_out_b, xp2, gt_a, sc_f, sh_f, g_ffn, seq)
    y_p = _ffn(h2_p, wg_b, wu_b, wd_b, x1_p, gt_f, g_final, seq)
    oa_s = o_nsa_s.reshape(n_s * t_s, H_A * DH_A).astype(BF16)
    x1_s, h2_s = _outproj(oa_s, o_mla_s.astype(BF16), w_out_b, xs2, gt_a_s, sc_f_s, sh_f_s, g_ffn, t_s)
    y_s = _ffn(h2_s, wg_b, wu_b, wd_b, x1_s, gt_f_s, g_final, t_s)

    win_all = jnp.concatenate([state_win_kv, win_s.reshape(n_s, t_s, 2, DH_A)], axis=1)
    return (y_p.reshape(n_p, seq, d), y_s.reshape(n_s, t_s, d),
            nsa_p.reshape(n_p, seq, 4, DH_A), mla_p.reshape(n_p, seq, MLA_ROW),
            win_p.reshape(n_p, seq, 2 * DH_A)[:, -min(WINDOW, seq):].reshape(n_p, -1, 2, DH_A),
            nsa_s.reshape(n_s, t_s, 4, DH_A), mla_s.reshape(n_s, t_s, MLA_ROW),
            win_all[:, -min(WINDOW, wb + t_s):])
```

```python
import functools
import math

import numpy as np
import jax
import jax.numpy as jnp
from jax import lax
from jax.experimental import pallas as pl
from jax.experimental.pallas import tpu as pltpu

F32 = jnp.float32
BF16 = jnp.bfloat16
I32 = jnp.int32

H_A = 8
DH_A = 128
BLK = 64
N_SEL = 16
WINDOW = 512
H_B = 8
D_NOPE = 128
D_ROPE = 64
D_V = 128
Q_LORA = 512
KV_LORA = 512
MLA_ROW = KV_LORA + D_ROPE
PAGE_SIZE = 128
ROPE_THETA = 10000.0
N_BUCKETS = 32
T5_MAX_DIST = 128
LOG2E = math.log2(math.e)
QSCALE_A = DH_A ** -0.5 * LOG2E
QSCALE_B = (D_NOPE + D_ROPE) ** -0.5 * LOG2E
EPS = 1e-6
NEG_INF = -1e30
FORCED_SCORE = 1e4

Q_TILE = 128
K_TILE = 512
MLA_K_TILE = 512
NEW_BLOCK_SLOT = 2
LANES = 128
V7X_VMEM_LIMIT = 56 * 1024 * 1024
NSA_ROWS = 4
PAGE_ROWS = PAGE_SIZE * NSA_ROWS
BLK_ROWS = BLK * NSA_ROWS
CHUNK_PAGES = 32
CMP_RING = 3
MLA_RING = 4
DMA_UNROLL = 8


def _t5_bucket_np(n):
    n = np.maximum(np.asarray(n, np.int64), 0)
    exact = N_BUCKETS // 2
    nf = np.maximum(n, 1).astype(np.float32)
    far = exact + (np.log(nf / np.float32(exact)) / np.float32(math.log(T5_MAX_DIST / exact))
                   * np.float32(N_BUCKETS - exact)).astype(np.int32)
    return np.where(n < exact, n, np.minimum(far, N_BUCKETS - 1)).astype(np.int32)


_BUCKET_START = [int(np.argmax(_t5_bucket_np(np.arange(4 * T5_MAX_DIST)) >= b)) for b in range(N_BUCKETS)]


def _dot(a, b):
    return jnp.dot(a, b, preferred_element_type=F32)


def _dot_nt(a, b):
    return lax.dot_general(a, b, (((1,), (1,)), ((), ())), preferred_element_type=F32)


def _rms(x, g):
    return x * lax.rsqrt(jnp.mean(x * x, axis=-1, keepdims=True) + EPS) * g


def _cparams(sem, vmem=V7X_VMEM_LIMIT):
    return pltpu.CompilerParams(dimension_semantics=sem, vmem_limit_bytes=vmem)


def _const_spec(shape):
    nd = len(shape)
    return pl.BlockSpec(shape, lambda *_: (0,) * nd, pipeline_mode=pl.Buffered(1))


def _smem_spec():
    return pl.BlockSpec(memory_space=pltpu.SMEM)


def _bias_from_smem(n, tbl_ref, h, offset=0.0):
    bias = jnp.full(n.shape, 0.0, F32) + (tbl_ref[N_BUCKETS - 1, h] - offset) * LOG2E
    for b in range(N_BUCKETS - 2, -1, -1):
        bias = jnp.where(n < _BUCKET_START[b + 1], (tbl_ref[b, h] - offset) * LOG2E, bias)
    return bias


def _bias_from_cols(n, tbl_ref):
    bias = jnp.broadcast_to(tbl_ref[:, N_BUCKETS - 1:N_BUCKETS], (tbl_ref.shape[0], n.shape[1]))
    for b in range(N_BUCKETS - 2, -1, -1):
        bias = jnp.where(n < _BUCKET_START[b + 1], tbl_ref[:, b:b + 1], bias)
    return bias * LOG2E


def _mod_kernel(c_ref, w_ref, b_ref, o_ref):
    c = c_ref[...]
    s = (c * jax.nn.sigmoid(c)).astype(BF16)
    o_ref[...] = _dot(s, w_ref[...].astype(BF16)) + b_ref[...]


def _modulation(c_all, w_ada, b_ada):
    m, d = c_all.shape
    n = w_ada.shape[1]
    tn = 512
    return pl.pallas_call(
        _mod_kernel,
        grid=(n // tn,),
        in_specs=[pl.BlockSpec((m, d), lambda j: (0, 0)),
                  pl.BlockSpec((d, tn), lambda j: (0, j)),
                  pl.BlockSpec((1, tn), lambda j: (0, j))],
        out_specs=pl.BlockSpec((m, tn), lambda j: (0, j)),
        out_shape=jax.ShapeDtypeStruct((m, n), F32),
        compiler_params=_cparams(("arbitrary",)),
        name="adaln_mod",
    )(c_all, w_ada, b_ada.reshape(1, n))


C_QN = 0
C_KV = C_QN + H_A * DH_A
C_WIN = C_KV + 4 * DH_A
C_CQ = C_WIN + 2 * DH_A
C_CKV = C_CQ + Q_LORA
C_KR = C_CKV + KV_LORA
C_KRR = C_KR + LANES
C_G = C_KRR + LANES
C_END = C_G + LANES


def _inproj_kernel(x_ref, sc_ref, sh_ref, g_ref, w_ref, gq_ref, wuq_ref, gkv_ref, wkv_ref, cos_ref, sin_ref,
                   qn_ref, nsa_ref, win_ref, gates_ref, mla_ref, qnope_ref, qrope_ref, krp_ref, knope_ref, v_ref,
                   kvb_ref):
    x = x_ref[...]
    h = _rms(x, g_ref[...]) * (1.0 + sc_ref[...]) + sh_ref[...]
    z = _dot(h.astype(BF16), w_ref[...])
    qn_ref[...] = (z[:, C_QN:C_KV] * QSCALE_A).astype(BF16)
    nsa_ref[...] = z[:, C_KV:C_WIN]
    win_ref[...] = z[:, C_WIN:C_CQ]
    kvb_ref[...] = z[:, C_KV:C_CQ].astype(BF16)
    gates_ref[...] = jax.nn.sigmoid(z[:, C_G:C_END])
    cos = cos_ref[...]
    sin = sin_ref[...]
    kvn = _rms(z[:, C_CKV:C_KR], gkv_ref[...])
    krp = z[:, C_KR:C_KRR] * cos + z[:, C_KRR:C_G] * sin
    mla_ref[:, 0:KV_LORA] = kvn
    mla_ref[:, KV_LORA:MLA_ROW] = krp[:, 0:D_ROPE]
    krp_ref[...] = krp.astype(BF16)
    cqn = _rms(z[:, C_CQ:C_CKV], gq_ref[...])
    q = _dot(cqn.astype(BF16), wuq_ref[...])
    nq = H_B * D_NOPE
    qnope_ref[...] = (q[:, 0:nq] * QSCALE_B).astype(BF16)
    for hh in range(H_B):
        a = q[:, nq + hh * LANES: nq + (hh + 1) * LANES]
        b = q[:, 2 * nq + hh * LANES: 2 * nq + (hh + 1) * LANES]
        qrope_ref[:, hh * LANES:(hh + 1) * LANES] = ((a * cos + b * sin) * QSCALE_B).astype(BF16)
    kv = _dot(kvn.astype(BF16), wkv_ref[...])
    knope_ref[...] = kv[:, 0:nq].astype(BF16)
    v_ref[...] = kv[:, nq:2 * nq].astype(BF16)


def _mod_spec(arr, tm, rows_per_group):
    d = arr.shape[-1]
    if arr.shape[1] != 1:
        return pl.BlockSpec((None, tm, d), lambda i, *_: (0, i, 0))
    tps = rows_per_group // tm
    return pl.BlockSpec((None, 1, d), lambda i, *_: (i // tps, 0, 0))


def _inproj(x2d, sc, sh, g_attn, w_ext, g_q, wuq_ext, g_kv, wkv, cos_t, sin_t, rows_per_group):
    rows, d = x2d.shape
    tm = min(512, rows)
    nt = rows // tm
    mod_spec = _mod_spec(sc, tm, rows_per_group)
    ptiles = cos_t.shape[0] // tm
    pos_spec = pl.BlockSpec((tm, LANES), lambda i: (i % ptiles, 0))
    row = lambda n: pl.BlockSpec((tm, n), lambda i: (i, 0))
    nq = H_B * D_NOPE
    outs = [(H_A * DH_A, BF16), (4 * DH_A, F32), (2 * DH_A, F32), (LANES, F32), (MLA_ROW, F32),
            (nq, BF16), (H_B * LANES, BF16), (LANES, BF16), (nq, BF16), (nq, BF16), (6 * DH_A, BF16)]
    return pl.pallas_call(
        _inproj_kernel,
        grid=(nt,),
        in_specs=[row(d), mod_spec, mod_spec, _const_spec((1, d)), _const_spec(w_ext.shape),
                  _const_spec((1, Q_LORA)), _const_spec(wuq_ext.shape), _const_spec((1, KV_LORA)),
                  _const_spec(wkv.shape), pos_spec, pos_spec],
        out_specs=[row(n) for n, _ in outs],
        out_shape=[jax.ShapeDtypeStruct((rows, n), dt) for n, dt in outs],
        compiler_params=_cparams(("arbitrary",)),
        name="inproj",
    )(x2d, sc, sh, g_attn.reshape(1, d), w_ext, g_q.reshape(1, Q_LORA), wuq_ext, g_kv.reshape(1, KV_LORA),
      wkv, cos_t, sin_t)


def _compress_kernel(kb_ref, vb_ref, w1k_ref, w2k_ref, w1v_ref, w2v_ref, ck_ref, cv_ref):
    hk = jax.nn.gelu(_dot(kb_ref[...].astype(BF16), w1k_ref[...]))
    ck_ref[...] = _dot(hk.astype(BF16), w2k_ref[...])
    hv = jax.nn.gelu(_dot(vb_ref[...].astype(BF16), w1v_ref[...]))
    cv_ref[...] = _dot(hv.astype(BF16), w2v_ref[...])


def _compress_prompt(kb, vb, w1k, w2k, w1v, w2v):
    nblk = kb.shape[0]
    full = lambda a: pl.BlockSpec(a.shape, lambda i: (0,) * a.ndim)
    args = (kb, vb, w1k, w2k, w1v, w2v)
    return pl.pallas_call(
        _compress_kernel,
        grid=(1,),
        in_specs=[full(a) for a in args],
        out_specs=[pl.BlockSpec((nblk, DH_A), lambda i: (0, 0))] * 2,
        out_shape=[jax.ShapeDtypeStruct((nblk, DH_A), F32)] * 2,
        compiler_params=_cparams(("arbitrary",)),
        name="compress_prompt",
    )(*args)


def _flash_reset(m_ref, l_ref, a_ref):
    m_ref[...] = jnp.full(m_ref.shape, NEG_INF, F32)
    l_ref[...] = jnp.zeros(l_ref.shape, F32)
    a_ref[...] = jnp.zeros(a_ref.shape, F32)


def _flash_tile(m_ref, l_ref, a_ref, logits, v_of):
    probs = []
    for h, s in enumerate(logits):
        m = m_ref[h]
        m_new = jnp.maximum(m, jnp.max(s, axis=0, keepdims=True))
        alpha = jnp.exp2(m - m_new)
        p = jnp.exp2(s - m_new)
        l_ref[h] = alpha * l_ref[h] + jnp.sum(p, axis=0, keepdims=True)
        m_ref[h] = m_new
        probs.append((alpha, p.astype(BF16)))
    for h, (alpha, p) in enumerate(probs):
        a_ref[h] = alpha * a_ref[h] + _dot(v_of(h), p)


def _nsa_prompt_kernel(tbl_ref, q_ref, gt_ref, ck_ref, cvt_ref, ks_ref, vst_ref, kw_ref, vwt_ref, o_ref,
                       out_ref, m_ref, l_ref, a_ref, band_ref, cband_ref, score_ref, self_ref, selr_ref, madd_ref,
                       nearm_ref, *, nb, k_sel):
    qb = pl.program_id(1)
    qs = qb * Q_TILE
    t_iota = lax.broadcasted_iota(I32, (1, Q_TILE), 1)
    tpos = qs + t_iota
    b_iota = lax.broadcasted_iota(I32, (nb, 1), 0)
    kp_iota = lax.broadcasted_iota(I32, (2 * Q_TILE, 1), 0)

    @pl.when((pl.program_id(0) == 0) & (qb == 0))
    def _():
        n_band = jnp.maximum(t_iota - kp_iota + Q_TILE, 0)
        j_iota = lax.broadcasted_iota(I32, (cband_ref.shape[1], 1), 0)
        n_cb = jnp.maximum(t_iota - (BLK - 1) - BLK * (j_iota - 2), 0)
        for h in range(H_A):
            band_ref[h] = _bias_from_smem(n_band, tbl_ref, h, offset=tbl_ref[N_BUCKETS - 1, h])
            cband_ref[h] = _bias_from_smem(n_cb, tbl_ref, h)

    blk_end = b_iota * BLK + (BLK - 1)
    valid_c = blk_end <= tpos
    valid_cf = valid_c.astype(F32)
    j_rel = b_iota - 2 * qb + 2
    ck = ck_ref[...]
    cvt = cvt_ref[...]

    def cmp_bias(h):
        bias = jnp.full((nb, Q_TILE), 0.0, F32) + tbl_ref[N_BUCKETS - 1, h] * LOG2E
        for j in range(4):
            bias = jnp.where(j_rel == j, cband_ref[h, j:j + 1, :], bias)
        return bias

    imp = jnp.zeros((nb, Q_TILE), F32)
    lgs = [_dot_nt(ck, q_ref[:, h * DH_A:(h + 1) * DH_A]) for h in range(H_A)]
    ps = []
    for h in range(H_A):
        lg = jnp.where(valid_c, lgs[h] + cmp_bias(h), NEG_INF)
        m = jnp.max(lg, axis=0, keepdims=True)
        p = jnp.exp2(lg - m) * valid_cf
        p = p / jnp.maximum(jnp.sum(p, axis=0, keepdims=True), 1e-30)
        imp = imp + p
        ps.append(p.astype(BF16))
    for h in range(H_A):
        out_ref[h] = gt_ref[3 * h:3 * h + 1, :] * _dot(cvt, ps[h])

    cur = tpos // BLK
    forced = (b_iota == 0) | (b_iota == cur) | (b_iota == cur - 1)
    score = jnp.where(forced, FORCED_SCORE, jnp.where(b_iota <= cur, imp, -FORCED_SCORE))
    score_ref[...] = score

    def rank_body(bp, rank):
        row = score_ref[pl.ds(bp, 1), :]
        tie = jnp.where(bp < b_iota, 1, 0)
        return rank + jnp.where(row > score, 1, jnp.where(row == score, tie, 0))

    rank = lax.fori_loop(0, nb, rank_body, jnp.zeros((nb, Q_TILE), I32), unroll=4)
    sel = rank < k_sel
    selr_ref[...] = sel.astype(F32)
    self_ref[...] = jnp.where(sel & (b_iota < 2 * qb - 2), 0.0, NEG_INF)
    nfar = (jnp.maximum(qb - 1, 0) * Q_TILE + (K_TILE - 1)) // K_TILE
    blocks_per_tile = K_TILE // BLK

    def mask_body(j, c):
        for i in range(blocks_per_tile):
            row = self_ref[pl.ds(j * blocks_per_tile + i, 1), :]
            madd_ref[pl.ds(pl.multiple_of(j * K_TILE + i * BLK, BLK), BLK), :] = jnp.broadcast_to(row, (BLK, Q_TILE))
        return c

    lax.fori_loop(0, nfar, mask_body, 0)

    causal = (kp_iota - Q_TILE) <= t_iota
    in_seq = (kp_iota + qs) >= Q_TILE
    for i in range(4):
        bi = 2 * qb - 2 + i
        row = selr_ref[pl.ds(jnp.maximum(bi, 0), 1), :]
        ok = (row > 0.5) & causal[i * BLK:(i + 1) * BLK] & (bi >= 0)
        nearm_ref[i * BLK:(i + 1) * BLK, :] = jnp.where(ok, 0.0, NEG_INF)
    win_near = jnp.where(causal & in_seq, 0.0, NEG_INF)
    nwf = WINDOW - Q_TILE
    wf_iota = lax.broadcasted_iota(I32, (nwf, 1), 0)
    win_far = jnp.where((wf_iota >= t_iota) & ((wf_iota + qs) >= WINDOW), 0.0, NEG_INF)

    _flash_reset(m_ref, l_ref, a_ref)
    heads = range(H_A)
    q_of = lambda h: q_ref[:, h * DH_A:(h + 1) * DH_A]

    def far_body(j, c):
        r0 = pl.multiple_of(Q_TILE + j * K_TILE, Q_TILE)
        kt = ks_ref[pl.ds(r0, K_TILE), :]
        vt = vst_ref[:, pl.ds(r0, K_TILE)]
        mt = madd_ref[pl.ds(pl.multiple_of(j * K_TILE, K_TILE), K_TILE), :]
        _flash_tile(m_ref, l_ref, a_ref, [_dot_nt(kt, q_of(h)) + mt for h in heads], lambda h: vt)
        return c

    lax.fori_loop(0, nfar, far_body, 0)
    near0 = pl.multiple_of(qs, Q_TILE)
    kt = ks_ref[pl.ds(near0, 2 * Q_TILE), :]
    vt = vst_ref[:, pl.ds(near0, 2 * Q_TILE)]
    nearm = nearm_ref[...]
    _flash_tile(m_ref, l_ref, a_ref, [_dot_nt(kt, q_of(h)) + (band_ref[h] + nearm) for h in heads], lambda h: vt)
    for h in heads:
        out_ref[h] += gt_ref[3 * h + 1:3 * h + 2, :] * (a_ref[h] / jnp.maximum(l_ref[h], 1e-30))

    _flash_reset(m_ref, l_ref, a_ref)
    kt = kw_ref[pl.ds(near0, nwf), :]
    vt = vwt_ref[:, pl.ds(near0, nwf)]
    _flash_tile(m_ref, l_ref, a_ref, [_dot_nt(kt, q_of(h)) + win_far for h in heads], lambda h: vt)
    wn0 = pl.multiple_of(qs + nwf, Q_TILE)
    kt2 = kw_ref[pl.ds(wn0, 2 * Q_TILE), :]
    vt2 = vwt_ref[:, pl.ds(wn0, 2 * Q_TILE)]
    _flash_tile(m_ref, l_ref, a_ref, [_dot_nt(kt2, q_of(h)) + (band_ref[h] + win_near) for h in heads], lambda h: vt2)
    for h in heads:
        tot = out_ref[h] + gt_ref[3 * h + 2:3 * h + 3, :] * (a_ref[h] / jnp.maximum(l_ref[h], 1e-30))
        o_ref[:, h * DH_A:(h + 1) * DH_A] = tot.T.astype(BF16)


def _nsa_prompt(rel_table, qn, gates_t, ck, cvt, ks_p, vst_p, kw_p, vwt_p, n_seq, seq):
    nb = seq // BLK
    nq = seq // Q_TILE
    k_sel = min(N_SEL, nb)
    per_seq = lambda a: pl.BlockSpec((None,) + a.shape[1:], lambda n, i: (n,) + (0,) * (a.ndim - 1),
                                     pipeline_mode=pl.Buffered(1))
    kern = functools.partial(_nsa_prompt_kernel, nb=nb, k_sel=k_sel)
    stat = pltpu.VMEM((H_A, 1, Q_TILE), F32)
    return pl.pallas_call(
        kern,
        grid=(n_seq, nq),
        in_specs=[_smem_spec(),
                  pl.BlockSpec((Q_TILE, H_A * DH_A), lambda n, i: (n * nq + i, 0)),
                  pl.BlockSpec((None, 3 * H_A, Q_TILE), lambda n, i: (n, 0, i)),
                  per_seq(ck), per_seq(cvt), per_seq(ks_p), per_seq(vst_p), per_seq(kw_p), per_seq(vwt_p)],
        out_specs=pl.BlockSpec((Q_TILE, H_A * DH_A), lambda n, i: (n * nq + i, 0)),
        out_shape=jax.ShapeDtypeStruct((n_seq * seq, H_A * DH_A), BF16),
        scratch_shapes=[pltpu.VMEM((H_A, DH_A, Q_TILE), F32), stat, stat, pltpu.VMEM((H_A, DH_A, Q_TILE), F32),
                        pltpu.VMEM((H_A, 2 * Q_TILE, Q_TILE), F32), pltpu.VMEM((H_A, 8, Q_TILE), F32),
                        pltpu.VMEM((nb, Q_TILE), F32), pltpu.VMEM((nb, Q_TILE), F32), pltpu.VMEM((nb, Q_TILE), F32),
                        pltpu.VMEM((seq, Q_TILE), F32), pltpu.VMEM((2 * Q_TILE, Q_TILE), F32)],
        compiler_params=_cparams(("arbitrary", "arbitrary")),
        name="nsa_prompt",
    )(rel_table, qn, gates_t, ck, cvt, ks_p, vst_p, kw_p, vwt_p)


def _mla_prompt_kernel(qn_ref, qr_ref, kn_ref, kr_ref, vt_ref, o_ref, m_ref, l_ref, a_ref, *, tk):
    qb = pl.program_id(1)
    qs = qb * Q_TILE
    nfull = qs // tk
    t_iota = lax.broadcasted_iota(I32, (1, Q_TILE), 1)
    k_iota = lax.broadcasted_iota(I32, (tk, 1), 0)
    diag0 = pl.multiple_of(nfull * tk, tk)
    diag_mask = jnp.where((k_iota + diag0) <= (t_iota + qs), 0.0, NEG_INF)
    _flash_reset(m_ref, l_ref, a_ref)

    def tile(r0, mask):
        kr = kr_ref[pl.ds(r0, tk), :]
        logits = []
        for h in range(H_B):
            kh = jnp.concatenate([kn_ref[pl.ds(r0, tk), h * D_NOPE:(h + 1) * D_NOPE], kr], axis=1)
            qh = jnp.concatenate([qn_ref[:, h * D_NOPE:(h + 1) * D_NOPE], qr_ref[:, h * LANES:(h + 1) * LANES]], axis=1)
            s = _dot_nt(kh, qh)
            logits.append(s if mask is None else s + mask)
        _flash_tile(m_ref, l_ref, a_ref, logits, lambda h: vt_ref[h * D_V:(h + 1) * D_V, pl.ds(r0, tk)])

    def body(j, c):
        tile(pl.multiple_of(j * tk, tk), None)
        return c

    lax.fori_loop(0, nfull, body, 0)
    tile(diag0, diag_mask)
    for h in range(H_B):
        o_ref[:, h * D_V:(h + 1) * D_V] = (a_ref[h] / jnp.maximum(l_ref[h], 1e-30)).T.astype(BF16)


def _mla_prompt(qnope, qrope, knope, krp, vt, n_seq, seq):
    nq = seq // Q_TILE
    tk = min(MLA_K_TILE, seq)
    per_seq = lambda a: pl.BlockSpec((None,) + a.shape[1:], lambda n, i: (n,) + (0,) * (a.ndim - 1),
                                     pipeline_mode=pl.Buffered(1))
    qspec = lambda w: pl.BlockSpec((Q_TILE, w), lambda n, i: (n * nq + i, 0))
    stat = pltpu.VMEM((H_B, 1, Q_TILE), F32)
    return pl.pallas_call(
        functools.partial(_mla_prompt_kernel, tk=tk),
        grid=(n_seq, nq),
        in_specs=[qspec(H_B * D_NOPE), qspec(H_B * LANES), per_seq(knope), per_seq(krp), per_seq(vt)],
        out_specs=qspec(H_B * D_V),
        out_shape=jax.ShapeDtypeStruct((n_seq * seq, H_B * D_V), BF16),
        scratch_shapes=[stat, stat, pltpu.VMEM((H_B, D_V, Q_TILE), F32)],
        compiler_params=_cparams(("arbitrary", "arbitrary")),
        name="mla_prompt",
    )(qnope, qrope, knope, krp, vt)


def _outproj_kernel(oa_ref, ob_ref, w_ref, x_ref, gt_ref, sc_ref, sh_ref, g_ref, x1_ref, h_ref):
    half = oa_ref.shape[1]
    mixed = _dot(oa_ref[...], w_ref[0:half, :]) + _dot(ob_ref[...], w_ref[half:2 * half, :])
    x1 = x_ref[...] + gt_ref[...] * mixed
    x1_ref[...] = x1
    h_ref[...] = (_rms(x1, g_ref[...]) * (1.0 + sc_ref[...]) + sh_ref[...]).astype(BF16)


def _outproj(oa, ob, w_out, x2d, gt, sc, sh, g_ffn, rows_per_group):
    rows, d = x2d.shape
    tm = min(512, rows)
    row = lambda n: pl.BlockSpec((tm, n), lambda i: (i, 0))
    ms = _mod_spec(gt, tm, rows_per_group)
    return pl.pallas_call(
        _outproj_kernel,
        grid=(rows // tm,),
        in_specs=[row(oa.shape[1]), row(ob.shape[1]), _const_spec(w_out.shape), row(d), ms, ms, ms,
                  _const_spec((1, d))],
        out_specs=[row(d), row(d)],
        out_shape=[jax.ShapeDtypeStruct((rows, d), F32), jax.ShapeDtypeStruct((rows, d), BF16)],
        compiler_params=_cparams(("arbitrary",)),
        name="outproj",
    )(oa, ob, w_out, x2d, gt, sc, sh, g_ffn.reshape(1, d))


def _ffn_kernel(h_ref, wg_ref, wu_ref, wd_ref, x1_ref, gt_ref, g_ref, y_ref, acc_ref):
    j = pl.program_id(1)
    h = h_ref[...]
    a = _dot(h, wg_ref[...])
    u = _dot(h, wu_ref[...])
    part = _dot(((a * jax.nn.sigmoid(a)) * u).astype(BF16), wd_ref[...])

    @pl.when(j == 0)
    def _():
        acc_ref[...] = part

    @pl.when(j > 0)
    def _():
        acc_ref[...] += part

    @pl.when(j == pl.num_programs(1) - 1)
    def _():
        x2 = x1_ref[...] + gt_ref[...] * acc_ref[...]
        y_ref[...] = _rms(x2, g_ref[...])


def _ffn(h2, wg, wu, wd, x1, gt, g_final, rows_per_group):
    rows, d = x1.shape
    dff = wg.shape[1]
    tm = min(512, rows)
    tf = 512
    row = lambda n: pl.BlockSpec((tm, n), lambda i, j: (i, 0))
    return pl.pallas_call(
        _ffn_kernel,
        grid=(rows // tm, dff // tf),
        in_specs=[row(d), pl.BlockSpec((d, tf), lambda i, j: (0, j)), pl.BlockSpec((d, tf), lambda i, j: (0, j)),
                  pl.BlockSpec((tf, d), lambda i, j: (j, 0)), row(d), _mod_spec(gt, tm, rows_per_group),
                  pl.BlockSpec((1, d), lambda i, j: (0, 0))],
        out_specs=row(d),
        out_shape=jax.ShapeDtypeStruct((rows, d), F32),
        scratch_shapes=[pltpu.VMEM((tm, d), F32)],
        compiler_params=_cparams(("arbitrary", "arbitrary")),
        name="ffn",
    )(h2, wg, wu, wd, x1, gt, g_final.reshape(1, d))


def _headproj_kernel(x_ref, w_ref, o_ref):
    o_ref[...] = _dot(x_ref[...].astype(BF16), w_ref[...])


def _headproj(x, w):
    rows = x.shape[0]
    nh, k, n = w.shape
    return pl.pallas_call(
        _headproj_kernel,
        grid=(nh,),
        in_specs=[pl.BlockSpec((rows, k), lambda h: (0, h)), pl.BlockSpec((None, k, n), lambda h: (h, 0, 0))],
        out_specs=pl.BlockSpec((rows, n), lambda h: (0, h)),
        out_shape=jax.ShapeDtypeStruct((rows, nh * n), F32),
        compiler_params=_cparams(("arbitrary",)),
        name="headproj",
    )(x, w)


def _chunk_page_copy(cache_ref, pt_ref, buf_ref, sem_ref, step, nchunk, cpages, ring, i):
    seq = step // nchunk
    c = step % nchunk
    slot = step % ring
    page = pt_ref[seq, c * cpages + i]
    return pltpu.make_async_copy(cache_ref.at[pl.ds(pl.multiple_of(page * PAGE_ROWS, PAGE_ROWS), PAGE_ROWS), :],
                                 buf_ref.at[slot, pl.ds(pl.multiple_of(i * PAGE_ROWS, PAGE_ROWS), PAGE_ROWS), :],
                                 sem_ref.at[slot])


def _ring_schedule(start_chunk, s, nstep, ring):
    ahead = ring - 1

    @pl.when(s == 0)
    def _():
        for d in range(ahead):
            @pl.when(d < nstep)
            def _():
                start_chunk(d)

    @pl.when(s + ahead < nstep)
    def _():
        start_chunk(s + ahead)


def _cmp_sample_kernel(pt_ref, cache_ref, q_ref, w1k_ref, w2k_ref, w1v_ref, w2v_ref, tbl_ref,
                       ocmp_ref, score_ref, buf_ref, sem_ref, stage_ref, *, nchunk, cpages, nbk, pitch, past):
    s = pl.program_id(0)
    nstep = pl.num_programs(0)
    c = s % nchunk
    slot = s % CMP_RING

    def start_chunk(step):
        def body(i, cc):
            _chunk_page_copy(cache_ref, pt_ref, buf_ref, sem_ref, step, nchunk, cpages, CMP_RING, i).start()
            return cc
        lax.fori_loop(0, cpages, body, 0, unroll=DMA_UNROLL)

    _ring_schedule(start_chunk, s, nstep, CMP_RING)

    def wait_body(i, cc):
        _chunk_page_copy(cache_ref, pt_ref, buf_ref, sem_ref, s, nchunk, cpages, CMP_RING, i).wait()
        return cc
    lax.fori_loop(0, cpages, wait_body, 0, unroll=DMA_UNROLL)

    for i in range(cpages):
        for half in range(2):
            blk = (c * cpages + i) * 2 + half
            for comp in range(2):
                slab = buf_ref[slot, pl.ds(i * PAGE_ROWS + half * BLK_ROWS + comp, BLK, stride=NSA_ROWS), :]
                stage_ref[comp, pl.ds(blk, BLK, stride=pitch), :] = slab

    @pl.when(c == nchunk - 1)
    def _():
        group = 16

        def compress(comp, w1_ref, w2_ref):
            pre = jnp.zeros((nbk, DH_A), F32)
            for g in range(BLK // group):
                x = jnp.concatenate([stage_ref[comp, pl.ds((g * group + j) * pitch, nbk), :] for j in range(group)],
                                    axis=1).astype(BF16)
                pre = pre + _dot(x, w1_ref[g * group * DH_A:(g + 1) * group * DH_A, :])
            return _dot(jax.nn.gelu(pre).astype(BF16), w2_ref[...]).astype(BF16)

        ck = compress(0, w1k_ref, w2k_ref)
        cv = compress(1, w1v_ref, w2v_ref)
        q = q_ref[...]
        rows = q.shape[0]
        nt = rows // H_A
        t_of_row = lax.broadcasted_iota(I32, (rows, 1), 0) // H_A
        b_iota = lax.broadcasted_iota(I32, (1, nbk), 1)
        n_cmp = jnp.maximum(past + t_of_row - (b_iota * BLK + (BLK - 1)), 0)
        lg = _dot_nt(q, ck) + _bias_from_cols(n_cmp, tbl_ref)
        m = jnp.max(lg, axis=-1, keepdims=True)
        p = jnp.exp2(lg - m)
        p = p / jnp.maximum(jnp.sum(p, axis=-1, keepdims=True), 1e-30)
        ocmp_ref[...] = _dot(p.astype(BF16), cv)

        width = score_ref.shape[-1]
        bid = lax.broadcasted_iota(I32, (1, width), 1)
        imp = jnp.concatenate([jnp.sum(p[t * H_A:(t + 1) * H_A], axis=0, keepdims=True) for t in range(nt)], axis=0)
        imp = jnp.concatenate([imp, jnp.zeros((nt, width - nbk), F32)], axis=1)
        cur = nbk
        forced = (bid == 0) | (bid == cur) | (bid == cur - 1)
        score = jnp.where(forced, FORCED_SCORE, imp)
        score_ref[...] = jnp.where(bid > cur, NEG_INF, score)


def _select_kernel(s_ref, sel_ref, *, nbs, k_sel):
    s = s_ref[...]
    b_iota = lax.broadcasted_iota(I32, (s.shape[0], 1), 0)

    def rank_body(bp, rank):
        row = s_ref[pl.ds(bp, 1), :]
        tie = jnp.where(bp < b_iota, 1, 0)
        return rank + jnp.where(row > s, 1, jnp.where(row == s, tie, 0))

    rank = lax.fori_loop(0, nbs, rank_body, jnp.zeros(s.shape, I32), unroll=4)
    rows = [jnp.sum(jnp.where(rank == r, b_iota, 0), axis=0, keepdims=True) for r in range(k_sel)]
    rows += [jnp.zeros_like(rows[0])] * (sel_ref.shape[0] - k_sel)
    sel_ref[...] = jnp.concatenate(rows, axis=0)


def _select_blocks(score_t, nbs, k_sel):
    width, rows = score_t.shape
    return pl.pallas_call(
        functools.partial(_select_kernel, nbs=nbs, k_sel=k_sel),
        grid=(rows // LANES,),
        in_specs=[pl.BlockSpec((width, LANES), lambda i: (0, i))],
        out_specs=pl.BlockSpec((N_SEL, LANES), lambda i: (0, i)),
        out_shape=jax.ShapeDtypeStruct((N_SEL, rows), I32),
        compiler_params=_cparams(("arbitrary",)),
        name="select_blocks",
    )(score_t)


def _cmp_sample(page_table, cache2d, q32, w1k, w2k, w1v, w2v, tbl_rows, past):
    nseq, npages = page_table.shape
    rows = q32.shape[1]
    nt = rows // H_A
    nbk = past // BLK
    width = -(-(nbk + 1) // LANES) * LANES
    cpages = min(CHUNK_PAGES, npages)
    nchunk = npages // cpages
    pitch = nbk + 8
    kern = functools.partial(_cmp_sample_kernel, nchunk=nchunk, cpages=cpages, nbk=nbk, pitch=pitch, past=past)
    cst = lambda a: pl.BlockSpec(a.shape, lambda s, pt: (0,) * a.ndim, pipeline_mode=pl.Buffered(1))
    per = lambda a: pl.BlockSpec((None,) + a.shape[1:], lambda s, pt: (s // nchunk,) + (0,) * (a.ndim - 1))
    grid_spec = pltpu.PrefetchScalarGridSpec(
        num_scalar_prefetch=1,
        grid=(nseq * nchunk,),
        in_specs=[pl.BlockSpec(memory_space=pl.ANY), per(q32), cst(w1k), cst(w2k), cst(w1v), cst(w2v), cst(tbl_rows)],
        out_specs=[pl.BlockSpec((None, rows, DH_A), lambda s, pt: (s // nchunk, 0, 0)),
                   pl.BlockSpec((None, nt, width), lambda s, pt: (s // nchunk, 0, 0))],
        scratch_shapes=[pltpu.VMEM((CMP_RING, cpages * PAGE_ROWS, DH_A), F32), pltpu.SemaphoreType.DMA((CMP_RING,)),
                        pltpu.VMEM((2, BLK * pitch, DH_A), F32)],
    )
    return pl.pallas_call(
        kern,
        grid_spec=grid_spec,
        out_shape=[jax.ShapeDtypeStruct((nseq, rows, DH_A), F32), jax.ShapeDtypeStruct((nseq, nt, width), F32)],
        compiler_params=_cparams(("arbitrary",)),
        name="cmp_sample",
    )(page_table, cache2d, q32, w1k, w2k, w1v, w2v, tbl_rows)


def _slc_copy(cache_ref, pt_ref, sel_ref, buf_ref, sem_ref, seq, slot, i, nbk):
    blk = jnp.minimum(sel_ref[seq, i], nbk - 1)
    r0 = pl.multiple_of(pt_ref[seq, blk // 2] * PAGE_ROWS + (blk % 2) * BLK_ROWS, BLK_ROWS)
    return pltpu.make_async_copy(cache_ref.at[pl.ds(r0, BLK_ROWS), :], buf_ref.at[slot, i], sem_ref.at[slot])


def _slc_sample_kernel(pt_ref, sel_ref, cache_ref, q_ref, rel_ref, new_ref, state_ref, wnew_ref, tbl_ref,
                       ocmp_ref, g0_ref, g1_ref, g2_ref, o_ref, buf_ref, sem_ref, *, nbk, nslot, nt, wb):
    b = pl.program_id(0)
    nseq = pl.num_programs(0)
    slot = b % 2
    ntot = nt * nslot

    def start_all(seq, sl):
        def body(i, c):
            _slc_copy(cache_ref, pt_ref, sel_ref, buf_ref, sem_ref, seq, sl, i, nbk).start()
            return c
        lax.fori_loop(0, ntot, body, 0, unroll=DMA_UNROLL)

    @pl.when(b == 0)
    def _():
        start_all(0, 0)

    @pl.when(b + 1 < nseq)
    def _():
        start_all(b + 1, 1 - slot)

    def wait_body(i, c):
        _slc_copy(cache_ref, pt_ref, sel_ref, buf_ref, sem_ref, b, slot, i, nbk).wait()
        return c
    lax.fori_loop(0, ntot, wait_body, 0, unroll=DMA_UNROLL)

    for t in range(nt):
        i_new = t * nslot + NEW_BLOCK_SLOT

        @pl.when(sel_ref[b, i_new] >= nbk)
        def _():
            buf_ref[slot, i_new] = new_ref[...]

    q = q_ref[...]
    rows = q.shape[0]
    t_of_row = lax.broadcasted_iota(I32, (rows, 1), 0) // H_A

    kst = state_ref[pl.ds(0, wb, stride=2), :].astype(BF16)
    vst = state_ref[pl.ds(1, wb, stride=2), :].astype(BF16)
    nwn = wnew_ref.shape[0] // 2
    kwn = wnew_ref[pl.ds(0, nwn, stride=2), :].astype(BF16)
    vwn = wnew_ref[pl.ds(1, nwn, stride=2), :].astype(BF16)
    rel_s = wb + t_of_row - lax.broadcasted_iota(I32, (1, wb), 1)
    rel_n = t_of_row - lax.broadcasted_iota(I32, (1, nwn), 1)
    ok_s = rel_s <= WINDOW
    ok_n = rel_n >= 0
    ls = jnp.where(ok_s, _dot_nt(q, kst) + _bias_from_cols(rel_s, tbl_ref), NEG_INF)
    ln = jnp.where(ok_n, _dot_nt(q, kwn) + _bias_from_cols(jnp.maximum(rel_n, 0), tbl_ref), NEG_INF)
    m = jnp.maximum(jnp.max(ls, axis=-1, keepdims=True), jnp.max(ln, axis=-1, keepdims=True))
    ps = jnp.exp2(ls - m) * ok_s.astype(F32)
    pn = jnp.exp2(ln - m) * ok_n.astype(F32)
    den = jnp.maximum(jnp.sum(ps, axis=-1, keepdims=True) + jnp.sum(pn, axis=-1, keepdims=True), 1e-30)
    o_win = _dot((ps / den).astype(BF16), vst) + _dot((pn / den).astype(BF16), vwn)

    blocks = buf_ref.at[slot]
    gather = lambda t, comp: jnp.concatenate(
        [blocks[t * nslot + r, pl.ds(comp, BLK, stride=NSA_ROWS), :] for r in range(nslot)], axis=0).astype(BF16)
    lgs = [_dot_nt(q[t * H_A:(t + 1) * H_A], gather(t, 2)) for t in range(nt)]
    probs = []
    for t in range(nt):
        rel = rel_ref[t:t + 1, :]
        ok = rel >= 0
        lg = jnp.where(ok, lgs[t] + _bias_from_cols(jnp.maximum(rel, 0), tbl_ref.at[0:H_A]), NEG_INF)
        m = jnp.max(lg, axis=-1, keepdims=True)
        p = jnp.exp2(lg - m) * ok.astype(F32)
        probs.append((p / jnp.maximum(jnp.sum(p, axis=-1, keepdims=True), 1e-30)).astype(BF16))
    o_slc = jnp.concatenate([_dot(probs[t], gather(t, 3)) for t in range(nt)], axis=0)
    o_ref[...] = g0_ref[...] * ocmp_ref[...] + g1_ref[...] * o_slc + g2_ref[...] * o_win


def _slc_sample(page_table, sel_flat, cache2d, q32, rel, new_blk, state, wnew, tbl_rows, ocmp, g0, g1, g2, nbk,
                nslot):
    nseq = page_table.shape[0]
    rows = q32.shape[1]
    nt = rows // H_A
    wb = state.shape[1] // 2
    per = lambda a: pl.BlockSpec((None,) + a.shape[1:], lambda b, *_: (b,) + (0,) * (a.ndim - 1))
    kern = functools.partial(_slc_sample_kernel, nbk=nbk, nslot=nslot, nt=nt, wb=wb)
    grid_spec = pltpu.PrefetchScalarGridSpec(
        num_scalar_prefetch=2,
        grid=(nseq,),
        in_specs=[pl.BlockSpec(memory_space=pl.ANY), per(q32), per(rel), per(new_blk), per(state), per(wnew),
                  pl.BlockSpec(tbl_rows.shape, lambda b, *_: (0, 0)), per(ocmp), per(g0), per(g1), per(g2)],
        out_specs=pl.BlockSpec((None, rows, DH_A), lambda b, *_: (b, 0, 0)),
        scratch_shapes=[pltpu.VMEM((2, nt * nslot, BLK_ROWS, DH_A), F32), pltpu.SemaphoreType.DMA((2,))],
    )
    return pl.pallas_call(
        kern,
        grid_spec=grid_spec,
        out_shape=jax.ShapeDtypeStruct((nseq, rows, DH_A), F32),
        compiler_params=_cparams(("arbitrary",)),
        name="slc_sample",
    )(page_table, sel_flat, cache2d, q32, rel, new_blk, state, wnew, tbl_rows, ocmp, g0, g1, g2)


def _mla_page_copy(cache_ref, pt_ref, buf_ref, sem_ref, step, nchunk, cpages, i):
    seq = step // nchunk
    c = step % nchunk
    slot = step % MLA_RING
    page = pt_ref[seq, c * cpages + i]
    return pltpu.make_async_copy(cache_ref.at[page],
                                 buf_ref.at[slot, :, pl.ds(pl.multiple_of(i * PAGE_SIZE, PAGE_SIZE), PAGE_SIZE)],
                                 sem_ref.at[slot])


def _mla_sample_kernel(pt_ref, cache_ref, q_ref, new_ref, o_ref, buf_ref, sem_ref, m_ref, l_ref, acc_ref,
                       *, nchunk, cpages):
    s = pl.program_id(0)
    nstep = pl.num_programs(0)
    c = s % nchunk
    slot = s % MLA_RING

    def start_chunk(step):
        def body(i, cc):
            _mla_page_copy(cache_ref, pt_ref, buf_ref, sem_ref, step, nchunk, cpages, i).start()
            return cc
        lax.fori_loop(0, cpages, body, 0, unroll=DMA_UNROLL)

    _ring_schedule(start_chunk, s, nstep, MLA_RING)
    q = q_ref[...]

    @pl.when(c == 0)
    def _():
        new = new_ref[...]
        sn = _dot_nt(q, new.astype(BF16))
        rows = sn.shape[0]
        t_of_row = lax.broadcasted_iota(I32, (rows, 1), 0) // H_B
        j = lax.broadcasted_iota(I32, (1, new.shape[0]), 1)
        ok = j <= t_of_row
        sn = jnp.where(ok, sn, NEG_INF)
        m = jnp.max(sn, axis=-1, keepdims=True)
        p = jnp.exp2(sn - m) * ok.astype(F32)
        m_ref[...] = m
        l_ref[...] = jnp.sum(p, axis=-1, keepdims=True)
        acc_ref[...] = _dot(p.astype(BF16), new[:, 0:KV_LORA].astype(BF16))

    def wait_body(i, cc):
        _mla_page_copy(cache_ref, pt_ref, buf_ref, sem_ref, s, nchunk, cpages, i).wait()
        return cc
    lax.fori_loop(0, cpages, wait_body, 0, unroll=DMA_UNROLL)

    kv = buf_ref[slot].astype(BF16)
    sc = _dot(q, kv)
    m_old = m_ref[...]
    m_new = jnp.maximum(m_old, jnp.max(sc, axis=-1, keepdims=True))
    alpha = jnp.exp2(m_old - m_new)
    p = jnp.exp2(sc - m_new)
    l_new = alpha * l_ref[...] + jnp.sum(p, axis=-1, keepdims=True)
    acc = alpha * acc_ref[...] + _dot_nt(p.astype(BF16), kv[0:KV_LORA, :])
    m_ref[...] = m_new
    l_ref[...] = l_new
    acc_ref[...] = acc

    @pl.when(c == nchunk - 1)
    def _():
        o_ref[...] = acc / jnp.maximum(l_new, 1e-30)


def _mla_sample(page_table, cache_t, q576, new_rows):
    nseq, npages = page_table.shape
    rows = q576.shape[1]
    cpages = min(CHUNK_PAGES, npages)
    nchunk = npages // cpages
    kern = functools.partial(_mla_sample_kernel, nchunk=nchunk, cpages=cpages)
    per = lambda a: pl.BlockSpec((None,) + a.shape[1:], lambda s, pt: (s // nchunk,) + (0,) * (a.ndim - 1))
    grid_spec = pltpu.PrefetchScalarGridSpec(
        num_scalar_prefetch=1,
        grid=(nseq * nchunk,),
        in_specs=[pl.BlockSpec(memory_space=pl.ANY), per(q576), per(new_rows)],
        out_specs=pl.BlockSpec((None, rows, KV_LORA), lambda s, pt: (s // nchunk, 0, 0)),
        scratch_shapes=[pltpu.VMEM((MLA_RING, MLA_ROW, cpages * PAGE_SIZE), F32), pltpu.SemaphoreType.DMA((MLA_RING,)),
                        pltpu.VMEM((rows, 1), F32), pltpu.VMEM((rows, 1), F32), pltpu.VMEM((rows, KV_LORA), F32)],
    )
    return pl.pallas_call(
        kern,
        grid_spec=grid_spec,
        out_shape=jax.ShapeDtypeStruct((nseq, rows, KV_LORA), F32),
        compiler_params=_cparams(("arbitrary",)),
        name="mla_sample",
    )(page_table, cache_t, q576, new_rows)


def _prep_w_in(w_in):
    d = w_in.shape[0]
    sizes = (H_A * DH_A, 4 * DH_A, 2 * DH_A, 3 * H_A, Q_LORA, KV_LORA, D_ROPE)
    offs = np.concatenate([[0], np.cumsum(sizes)])
    seg = lambda i: w_in[:, int(offs[i]):int(offs[i + 1])]
    zq, zkv, zwin, zg, zcq, zckv, zkr = (seg(i) for i in range(7))
    half = D_ROPE // 2
    zrot = jnp.concatenate([-zkr[:, half:], zkr[:, :half]], axis=1)
    pad = lambda a: jnp.pad(a, ((0, 0), (0, LANES - a.shape[1])))
    w_ext = jnp.concatenate([zq, zkv, zwin, zcq, zckv, pad(zkr), pad(zrot), pad(zg)], axis=1)
    assert w_ext.shape == (d, C_END)
    return w_ext.astype(BF16)


def _prep_w_uq(w_uq):
    r = w_uq.shape[0]
    w = w_uq.reshape(r, H_B, D_NOPE + D_ROPE)
    nope = w[:, :, :D_NOPE].reshape(r, H_B * D_NOPE)
    rope = w[:, :, D_NOPE:]
    half = D_ROPE // 2
    rot = jnp.concatenate([-rope[:, :, half:], rope[:, :, :half]], axis=2)
    padh = lambda a: jnp.pad(a, ((0, 0), (0, 0), (0, LANES - D_ROPE))).reshape(r, H_B * LANES)
    return jnp.concatenate([nope, padh(rope), padh(rot)], axis=1).astype(BF16)


def _rope_tables(pos):
    half = D_ROPE // 2
    inv = ROPE_THETA ** (-jnp.arange(half, dtype=F32) / half)
    ang = pos.astype(F32)[:, None] * inv[None, :]
    cos = jnp.cos(ang)
    sin = jnp.sin(ang)
    z = jnp.zeros((pos.shape[0], LANES - D_ROPE), F32)
    return jnp.concatenate([cos, cos, z], axis=1), jnp.concatenate([sin, sin, z], axis=1)


def kernel(x_prompt, x_sample, cache_nsa_kv, cache_mla, state_win_kv, page_table, c_prompt, c_sample, rel_table,
           g_attn, g_ffn, g_final, w_ada, b_ada, w_in, g_qnorm, w_uq, g_kvnorm, w_uk, w_uv, w_ck1, w_ck2, w_cv1,
           w_cv2, w_out, w_gate, w_up, w_down):
    n_p, seq, d = x_prompt.shape
    n_s, t_s, _ = x_sample.shape
    npages = page_table.shape[1]
    past = npages * PAGE_SIZE
    wb = state_win_kv.shape[1]
    n_pool = cache_nsa_kv.shape[0]
    assert seq % K_TILE == 0 and seq % MLA_K_TILE == 0 and seq >= WINDOW and wb == WINDOW and t_s <= 8
    assert cache_nsa_kv.shape[1:] == (PAGE_SIZE, NSA_ROWS, DH_A) and cache_mla.shape[1:] == (PAGE_SIZE, MLA_ROW)

    w_ext = _prep_w_in(w_in)
    wuq_ext = _prep_w_uq(w_uq)
    wkv = jnp.concatenate([w_uk.reshape(KV_LORA, H_B * D_NOPE), w_uv.reshape(KV_LORA, H_B * D_V)], axis=1).astype(BF16)
    w_out_b = w_out.astype(BF16)
    wg_b, wu_b, wd_b = w_gate.astype(BF16), w_up.astype(BF16), w_down.astype(BF16)
    w1k_b, w2k_b, w1v_b, w2v_b = (w.astype(BF16) for w in (w_ck1, w_ck2, w_cv1, w_cv2))

    n_c = n_p + n_s
    c_all = jnp.pad(jnp.concatenate([c_prompt, c_sample], axis=0), ((0, (-n_c) % 8), (0, 0)))
    mod = _modulation(c_all, w_ada, b_ada)
    mod_p = mod[:n_p].reshape(n_p, 6, 1, d)
    mod_s = jnp.repeat(mod[n_p:n_c].reshape(n_s, 6, d), t_s, axis=0).transpose(1, 0, 2)[:, None]
    sh_a, sc_a, gt_a, sh_f, sc_f, gt_f = (mod_p[:, i] for i in range(6))
    sh_a_s, sc_a_s, gt_a_s, sh_f_s, sc_f_s, gt_f_s = (mod_s[i] for i in range(6))

    cos_p, sin_p = _rope_tables(jnp.arange(seq, dtype=I32))
    pos_s = past + jnp.tile(jnp.arange(t_s, dtype=I32), n_s)
    cos_s, sin_s = _rope_tables(pos_s)
    xp2 = x_prompt.reshape(n_p * seq, d)
    xs2 = x_sample.reshape(n_s * t_s, d)
    (qn_p, nsa_p, win_p, gates_p, mla_p, qnope_p, qrope_p, krp_p, knope_p, v_p, kvb_p) = _inproj(
        xp2, sc_a, sh_a, g_attn, w_ext, g_qnorm, wuq_ext, g_kvnorm, wkv, cos_p, sin_p, seq)
    (qn_s, nsa_s, win_s, gates_s, mla_s, qnope_s, qrope_s, _krp_s, _kn_s, _v_s, _kvb_s) = _inproj(
        xs2, sc_a_s, sh_a_s, g_attn, w_ext, g_qnorm, wuq_ext, g_kvnorm, wkv, cos_s, sin_s, t_s)

    nb = seq // BLK
    kvb_p3 = kvb_p.reshape(n_p, seq, 6 * DH_A)
    part = lambda i: kvb_p3[:, :, i * DH_A:(i + 1) * DH_A]
    kb = part(0).reshape(n_p * nb, BLK * DH_A)
    vb = part(1).reshape(n_p * nb, BLK * DH_A)
    ck_p, cv_p = _compress_prompt(kb, vb, w1k_b, w2k_b, w1v_b, w2v_b)
    ck_p = ck_p.reshape(n_p, nb, DH_A).astype(BF16)
    cvt_p = cv_p.reshape(n_p, nb, DH_A).transpose(0, 2, 1).astype(BF16)
    gates_t = gates_p[:, :3 * H_A].reshape(n_p, seq, 3 * H_A).transpose(0, 2, 1)

    ks, vs, kw, vw = part(2), part(3), part(4), part(5)
    ks_p = jnp.pad(ks, ((0, 0), (Q_TILE, 0), (0, 0)))
    vst_p = jnp.pad(vs, ((0, 0), (Q_TILE, 0), (0, 0))).transpose(0, 2, 1)
    kw_p = jnp.pad(kw, ((0, 0), (WINDOW, 0), (0, 0)))
    vwt_p = jnp.pad(vw, ((0, 0), (WINDOW, 0), (0, 0))).transpose(0, 2, 1)
    o_nsa_p = _nsa_prompt(rel_table, qn_p, gates_t, ck_p, cvt_p, ks_p, vst_p, kw_p, vwt_p, n_p, seq)

    o_mla_p = _mla_prompt(qnope_p, qrope_p, knope_p.reshape(n_p, seq, H_B * D_NOPE), krp_p.reshape(n_p, seq, LANES),
                          v_p.reshape(n_p, seq, H_B * D_V).transpose(0, 2, 1), n_p, seq)

    rows_s = t_s * H_A
    nbk = past // BLK
    nbs = -(-(past + t_s) // BLK)
    assert nbs == nbk + 1 and nbk >= 2
    k_sel = min(N_SEL, nbs)
    assert k_sel > NEW_BLOCK_SLOT and FORCED_SCORE > H_A
    cache2d = cache_nsa_kv.reshape(n_pool * PAGE_ROWS, DH_A)
    cache_t = cache_mla.transpose(0, 2, 1)
    q32 = qn_s.reshape(n_s, rows_s, DH_A)
    tbl_rows = jnp.tile(rel_table.T, (t_s, 1))

    ocmp_s, score = _cmp_sample(page_table, cache2d, q32, w1k_b, w2k_b, w1v_b, w2v_b, tbl_rows, past)
    n_q = n_s * t_s
    score_t = jnp.pad(score.reshape(n_q, -1), ((0, (-n_q) % LANES), (0, 0))).T
    sel = _select_blocks(score_t, nbs, k_sel).T[:n_q, :k_sel].reshape(n_s, t_s, k_sel)

    qpos = past + jnp.arange(t_s, dtype=I32)
    tok = sel[..., None] * BLK + jnp.arange(BLK, dtype=I32)
    rel_slc = (qpos[None, :, None, None] - tok).reshape(n_s, t_s, k_sel * BLK)
    new_blk = jnp.pad(nsa_s.reshape(n_s, t_s * NSA_ROWS, DH_A), ((0, 0), (0, BLK_ROWS - t_s * NSA_ROWS), (0, 0)))
    wnew = jnp.pad(win_s.reshape(n_s, t_s * 2, DH_A), ((0, 0), (0, 16 - t_s * 2), (0, 0)))
    gs = gates_s[:, :3 * H_A].reshape(n_s, rows_s, 3)
    g0, g1, g2 = (jnp.broadcast_to(gs[:, :, i:i + 1], (n_s, rows_s, DH_A)) for i in range(3))
    o_nsa_s = _slc_sample(page_table, sel.reshape(n_s, t_s * k_sel), cache2d, q32, rel_slc, new_blk,
                          state_win_kv.reshape(n_s, wb * 2, DH_A), wnew, tbl_rows, ocmp_s, g0, g1, g2, nbk, k_sel)

    wuk_t = w_uk.transpose(1, 2, 0).astype(BF16)
    qlat = _headproj(qnope_s, wuk_t).reshape(n_s, rows_s, KV_LORA).astype(BF16)
    qrope_s3 = qrope_s.reshape(n_s, t_s, H_B, LANES)[..., :D_ROPE].reshape(n_s, rows_s, D_ROPE)
    q576 = jnp.concatenate([qlat, qrope_s3], axis=-1)
    mla_new = jnp.pad(mla_s.reshape(n_s, t_s, MLA_ROW), ((0, 0), (0, 8 - t_s), (0, 0)))
    o_lat = _mla_sample(page_table, cache_t, q576, mla_new)
    wuv_h = w_uv.transpose(1, 0, 2).astype(BF16)
    o_mla_s = _headproj(o_lat.reshape(n_s * t_s, H_B * KV_LORA), wuv_h)

    x1_p, h2_p = _outproj(o_nsa_p, o_mla_p, w_out_b, xp2, gt_a, sc_f, sh_f, g_ffn, seq)
    y_p = _ffn(h2_p, wg_b, wu_b, wd_b, x1_p, gt_f, g_final, seq)
    oa_s = o_nsa_s.reshape(n_s * t_s, H_A * DH_A).astype(BF16)
    x1_s, h2_s = _outproj(oa_s, o_mla_s.astype(BF16), w_out_b, xs2, gt_a_s, sc_f_s, sh_f_s, g_ffn, t_s)
    y_s = _ffn(h2_s, wg_b, wu_b, wd_b, x1_s, gt_f_s, g_final, t_s)

    win_all = jnp.concatenate([state_win_kv, win_s.reshape(n_s, t_s, 2, DH_A)], axis=1)
    return (y_p.reshape(n_p, seq, d), y_s.reshape(n_s, t_s, d),
            nsa_p.reshape(n_p, seq, 4, DH_A), mla_p.reshape(n_p, seq, MLA_ROW),
            win_p.reshape(n_p, seq, 2 * DH_A)[:, -min(WINDOW, seq):].reshape(n_p, -1, 2, DH_A),
            nsa_s.reshape(n_s, t_s, 4, DH_A), mla_s.reshape(n_s, t_s, MLA_ROW),
            win_all[:, -min(WINDOW, wb + t_s):])
```
